```python
import math
import jax, jax.numpy as jnp
from jax import lax
import numpy as np

D_MODEL = 2048
BATCH = 8
SEQ = 2048
DEPTH = 2

CTX_LEN = 256
GRID_W = 64
HEAD_DIM = 64
MIX_WIDTH = D_MODEL
GROUP_WIDTH = MIX_WIDTH // 4
RET_HEADS = GROUP_WIDTH // HEAD_DIM
RET_CHUNK = 128
CONV_CH = GROUP_WIDTH
SWA_HEADS = GROUP_WIDTH // HEAD_DIM
SWA_KV_HEADS = 2
SWA_WINDOW = 128
SWA_BLOCK = 128
NA_HEADS = GROUP_WIDTH // HEAD_DIM
NA_ROWS = 8
NA_COLS = 16
N_EXPERTS = 16
EXPERT_FF = D_MODEL // 2
EC_CAPACITY_FACTOR = 2
ROPE_BASE = 10000.0
EPS = 1e-6
NEG_INF = -1e30
F32 = jnp.float32

COL_SPLITS = (
    [GROUP_WIDTH] * 4
    + [CONV_CH] * 3
    + [SWA_HEADS * HEAD_DIM, SWA_KV_HEADS * HEAD_DIM, SWA_KV_HEADS * HEAD_DIM]
    + [NA_HEADS * HEAD_DIM] * 3
)
IN_COLS = 4 * GROUP_WIDTH + 3 * CONV_CH + (SWA_HEADS + 2 * SWA_KV_HEADS) * HEAD_DIM + 3 * NA_HEADS * HEAD_DIM

kernel_name = "hybrid_parallel_heads_diffusion_block"


def rmsnorm(x, g):
    xf = x.astype(F32)
    y = xf * lax.rsqrt(jnp.mean(xf * xf, axis=-1, keepdims=True) + EPS)
    return (y * g.astype(F32)).astype(x.dtype)


def head_groupnorm(y):
    yf = y.astype(F32)
    mu = jnp.mean(yf, axis=-1, keepdims=True)
    var = jnp.mean(jnp.square(yf - mu), axis=-1, keepdims=True)
    return (yf - mu) * lax.rsqrt(var + EPS)


def axial_rope(x):
    L, hd = x.shape[1], x.shape[-1]
    half = hd // 2
    nf = half // 2
    t = jnp.arange(L)
    row = (t // GRID_W).astype(F32)
    col = (t % GRID_W).astype(F32)
    inv = ROPE_BASE ** (-jnp.arange(nf, dtype=F32) / nf)
    ang = jnp.concatenate([row[:, None] * inv, col[:, None] * inv], axis=-1)[:, None, :]
    cos, sin = jnp.cos(ang), jnp.sin(ang)
    xf = x.astype(F32)
    x1, x2 = xf[..., :half], xf[..., half:]
    return jnp.concatenate([x1 * cos - x2 * sin, x1 * sin + x2 * cos], axis=-1).astype(x.dtype)


def retention_scan(q, k, v, log_gamma, s0, include_diag):
    B, T, H, d = q.shape
    n = T // RET_CHUNK
    q, k, v = q.astype(F32), k.astype(F32), v.astype(F32)

    def chunks(a):
        return a.reshape(B, n, RET_CHUNK, H, d).transpose(1, 0, 3, 2, 4)

    pos = jnp.arange(RET_CHUNK, dtype=F32)
    diff = pos[:, None] - pos[None, :]
    tri = diff >= 0 if include_diag else diff > 0
    lg = log_gamma.astype(F32)
    dmat = jnp.where(tri, jnp.exp(lg[:, None, None] * jnp.maximum(diff, 0.0)), 0.0)
    xi = jnp.exp(lg[:, None] * (pos + 1.0))[..., None]
    zeta = jnp.exp(lg[:, None] * (RET_CHUNK - 1.0 - pos))[..., None]
    cdecay = jnp.exp(lg * RET_CHUNK)[:, None, None]

    def step(state, inp):
        qi, ki, vi = inp
        inner = jnp.einsum('bhqd,bhkd->bhqk', qi, ki) * dmat
        y = jnp.einsum('bhqk,bhkv->bhqv', inner, vi) + jnp.einsum('bhqd,bhdv->bhqv', qi, state) * xi
        state = state * cdecay + jnp.einsum('bhkd,bhkv->bhdv', ki * zeta, vi)
        return state, y

    state, ys = lax.scan(step, s0, (chunks(q), chunks(k), chunks(v)))
    return ys.transpose(1, 0, 3, 2, 4).reshape(B, T, H, d), state


def retention_mixer(q, k, v, g, qc, kc, vc, gc, logit_f, logit_b, need_ctx):
    B, L, _ = q.shape
    kscale = HEAD_DIM ** -0.5

    def heads(a):
        return a.reshape(a.shape[0], a.shape[1], RET_HEADS, HEAD_DIM)

    q, k, v = axial_rope(heads(q)), axial_rope(heads(k)) * kscale, heads(v)
    qc, kc, vc = heads(qc), heads(kc) * kscale, heads(vc)
    lg_f = jax.nn.log_sigmoid(logit_f.astype(F32))
    lg_b = jax.nn.log_sigmoid(logit_b.astype(F32))
    s0 = jnp.zeros((B, RET_HEADS, HEAD_DIM, HEAD_DIM), F32)

    def flip(a):
        return a[:, ::-1]

    yc_f, st_f = retention_scan(qc, kc, vc, lg_f, s0, True)
    yc_b, st_b = retention_scan(flip(qc), flip(kc), flip(vc), lg_b, s0, False)
    y_f, _ = retention_scan(q, k, v, lg_f, st_f, True)
    y_b, _ = retention_scan(flip(q), flip(k), flip(v), lg_b, st_b, False)
    y = (jax.nn.silu(g.astype(F32)) * head_groupnorm(y_f + flip(y_b)).reshape(B, L, -1)).astype(g.dtype)
    yc = None
    if need_ctx:
        T = qc.shape[1]
        yc = (jax.nn.silu(gc.astype(F32)) * head_groupnorm(yc_f + flip(yc_b)).reshape(B, T, -1)).astype(gc.dtype)
    return y, yc


def short_conv(bg, cg, hin, w):
    u = cg * hin
    up = jnp.pad(u, ((0, 0), (1, 1), (0, 0)))
    y = up[:, :-2] * w[0] + up[:, 1:-1] * w[1] + up[:, 2:] * w[2]
    return bg * y


def dense_ctx_attention(qc, kc, vc, sink):
    B, T = qc.shape[0], qc.shape[1]
    s = jnp.einsum('bqkgd,bckd->bkgqc', qc, kc).astype(F32) * (HEAD_DIM ** -0.5)
    if sink is not None:
        s = jnp.concatenate([s, jnp.broadcast_to(sink.astype(F32)[None, :, :, None, None], s.shape[:-1] + (1,))], axis=-1)
    p = jax.nn.softmax(s, axis=-1)[..., :kc.shape[1]].astype(vc.dtype)
    return jnp.einsum('bkgqc,bckd->bqkgd', p, vc).reshape(B, T, -1)


def swa_mixer(q, k, v, qc, kc, vc, sink, need_ctx):
    B, L, _ = q.shape
    T = qc.shape[1]
    KV, G = SWA_KV_HEADS, SWA_HEADS // SWA_KV_HEADS
    scale = HEAD_DIM ** -0.5
    q = axial_rope(q.reshape(B, L, SWA_HEADS, HEAD_DIM)).reshape(B, L, KV, G, HEAD_DIM)
    k = axial_rope(k.reshape(B, L, KV, HEAD_DIM))
    v = v.reshape(B, L, KV, HEAD_DIM)
    qc = qc.reshape(B, T, KV, G, HEAD_DIM)
    kc = kc.reshape(B, T, KV, HEAD_DIM)
    vc = vc.reshape(B, T, KV, HEAD_DIM)
    nb = L // SWA_BLOCK
    qb = q.reshape(B, nb, SWA_BLOCK, KV, G, HEAD_DIM)

    def band(a):
        ap = jnp.pad(a, ((0, 0), (SWA_BLOCK, SWA_BLOCK), (0, 0), (0, 0))).reshape(B, nb + 2, SWA_BLOCK, KV, HEAD_DIM)
        return jnp.concatenate([ap[:, :-2], ap[:, 1:-1], ap[:, 2:]], axis=2)

    kw, vw = band(k), band(v)
    W = 3 * SWA_BLOCK
    blk = jnp.arange(nb)[:, None, None] * SWA_BLOCK
    qpos = blk + jnp.arange(SWA_BLOCK)[None, :, None]
    kpos = blk - SWA_BLOCK + jnp.arange(W)[None, None, :]
    valid = (jnp.abs(kpos - qpos) <= SWA_WINDOW) & (kpos >= 0) & (kpos < L)
    s_win = jnp.einsum('bnqkgd,bnjkd->bnkgqj', qb, kw).astype(F32) * scale
    s_win = jnp.where(valid[None, :, None, None], s_win, NEG_INF)
    s_ctx = jnp.einsum('bnqkgd,bckd->bnkgqc', qb, kc).astype(F32) * scale
    s_sink = jnp.broadcast_to(sink.astype(F32).reshape(KV, G)[None, None, :, :, None, None], s_win.shape[:-1] + (1,))
    p = jax.nn.softmax(jnp.concatenate([s_win, s_ctx, s_sink], axis=-1), axis=-1).astype(v.dtype)
    o = (jnp.einsum('bnkgqj,bnjkd->bnqkgd', p[..., :W], vw)
         + jnp.einsum('bnkgqc,bckd->bnqkgd', p[..., W:W + T], vc))
    y = o.reshape(B, L, SWA_HEADS * HEAD_DIM)
    yc = dense_ctx_attention(qc, kc, vc, sink.reshape(KV, G)) if need_ctx else None
    return y, yc


def na_mixer(q, k, v, qc, kc, vc, rpb, need_ctx):
    B, L, _ = q.shape
    T = qc.shape[1]
    H = NA_HEADS
    scale = HEAD_DIM ** -0.5
    rows = L // GRID_W
    wr, wc = min(NA_ROWS, rows), NA_COLS
    qg = q.reshape(B, rows, GRID_W, H, HEAD_DIM)
    kg = k.reshape(B, rows, GRID_W, H, HEAD_DIM)
    vg = v.reshape(B, rows, GRID_W, H, HEAD_DIM)
    kc4 = kc.reshape(B, T, H, HEAD_DIM)
    vc4 = vc.reshape(B, T, H, HEAD_DIM)
    r = jnp.arange(rows)
    row_idx = jnp.clip(r - wr // 2, 0, rows - wr)[:, None] + jnp.arange(wr)[None, :]
    nk = wr * GRID_W
    kb = kg[:, row_idx].reshape(B, rows, nk, H, HEAD_DIM)
    vb = vg[:, row_idx].reshape(B, rows, nk, H, HEAD_DIM)
    cq = jnp.arange(GRID_W)
    col_start = jnp.clip(cq - wc // 2, 0, GRID_W - wc)
    col_ok = (cq[None, :] >= col_start[:, None]) & (cq[None, :] < col_start[:, None] + wc)
    dr = row_idx - r[:, None] + (NA_ROWS - 1)
    dc = jnp.clip(cq[None, :] - cq[:, None], -(wc - 1), wc - 1) + (NA_COLS - 1)
    bias = rpb.astype(F32)[:, dr[:, :, None, None], dc[None, None, :, :]]
    bias = jnp.where(col_ok[None, None, None], bias, NEG_INF)
    bias = bias.transpose(1, 0, 3, 2, 4).reshape(rows, H, GRID_W, nk)
    s_nb = jnp.einsum('brqhd,brkhd->brhqk', qg, kb).astype(F32) * scale + bias[None]
    s_ctx = jnp.einsum('brqhd,bchd->brhqc', qg, kc4).astype(F32) * scale
    p = jax.nn.softmax(jnp.concatenate([s_nb, s_ctx], axis=-1), axis=-1).astype(v.dtype)
    o = (jnp.einsum('brhqk,brkhd->brqhd', p[..., :nk], vb)
         + jnp.einsum('brhqc,bchd->brqhd', p[..., nk:], vc4))
    y = o.reshape(B, L, H * HEAD_DIM)
    yc = dense_ctx_attention(qc.reshape(B, T, H, 1, HEAD_DIM), kc4, vc4, None) if need_ctx else None
    return y, yc


def split_cols(z):
    idx = [int(i) for i in np.cumsum(COL_SPLITS)[:-1]]
    return jnp.split(z, idx, axis=-1)


def token_mix(h, hc, w_in, w_out, logit_f, logit_b, conv_w, sink, rpb, need_ctx):
    p = split_cols(h @ w_in)
    pc = split_cols(hc @ w_in)
    y_ret, yc_ret = retention_mixer(*p[0:4], *pc[0:4], logit_f, logit_b, need_ctx)
    y_conv = short_conv(*p[4:7], conv_w)
    y_swa, yc_swa = swa_mixer(*p[7:10], *pc[7:10], sink, need_ctx)
    y_na, yc_na = na_mixer(*p[10:13], *pc[10:13], rpb, need_ctx)
    y = jnp.concatenate([y_ret, y_conv, y_swa, y_na], axis=-1) @ w_out
    yc = None
    if need_ctx:
        yc_conv = short_conv(*pc[4:7], conv_w)
        yc = jnp.concatenate([yc_ret, yc_conv, yc_swa, yc_na], axis=-1) @ w_out
    return y, yc


def expert_choice_ffn(h, w_router, w_gate, w_up, w_down):
    B, T, D = h.shape
    cap = EC_CAPACITY_FACTOR * T // N_EXPERTS
    aff = jax.nn.softmax((h @ w_router).astype(F32), axis=-1)
    gate, idx = lax.top_k(jnp.swapaxes(aff, 1, 2), cap)
    xs = jax.vmap(lambda hb, ib: hb[ib])(h, idx)
    a = jnp.einsum('becd,edf->becf', xs, w_gate)
    u = jnp.einsum('becd,edf->becf', xs, w_up)
    ye = jnp.einsum('becf,efd->becd', jax.nn.silu(a) * u, w_down) * gate[..., None].astype(h.dtype)
    return jax.vmap(lambda yb, ib: jnp.zeros((T, D), yb.dtype).at[ib.reshape(-1)].add(yb.reshape(-1, D)))(ye, idx)


def setup_inputs(seed: int = 0) -> dict:
    key = jax.random.key(seed)
    ks = jax.random.split(key, 24)
    D = D_MODEL
    nrm = jax.random.normal
    gamma0 = 1.0 - 2.0 ** (-5.0 - jnp.arange(RET_HEADS, dtype=F32))
    logit0 = jnp.log(gamma0) - jnp.log1p(-gamma0)
    return {
        "x": nrm(ks[0], (BATCH, SEQ, D), F32),
        "c": nrm(ks[1], (BATCH, D), F32),
        "ctx": nrm(ks[2], (BATCH, CTX_LEN, D), F32),
        "c_ctx": nrm(ks[3], (D,), F32),
        "w_ada": nrm(ks[4], (DEPTH, D, 6 * D), F32) * (0.5 * D ** -0.5),
        "b_ada": nrm(ks[5], (DEPTH, 6 * D), F32) * 0.01,
        "norm_mix": 1.0 + 0.02 * nrm(ks[6], (DEPTH, D), F32),
        "norm_ffn": 1.0 + 0.02 * nrm(ks[7], (DEPTH, D), F32),
        "w_in": nrm(ks[8], (DEPTH, D, IN_COLS), F32) * D ** -0.5,
        "w_out": nrm(ks[9], (DEPTH, MIX_WIDTH, D), F32) * MIX_WIDTH ** -0.5,
        "ret_decay_fwd": logit0[None] + 0.1 * nrm(ks[10], (DEPTH, RET_HEADS), F32),
        "ret_decay_bwd": logit0[None] + 0.1 * nrm(ks[11], (DEPTH, RET_HEADS), F32),
        "conv_w": nrm(ks[12], (DEPTH, 3, CONV_CH), F32) * 3.0 ** -0.5,
        "swa_sink": 0.5 * nrm(ks[13], (DEPTH, SWA_HEADS), F32),
        "na_rpb": 0.1 * nrm(ks[14], (DEPTH, NA_HEADS, 2 * NA_ROWS - 1, 2 * NA_COLS - 1), F32),
        "w_router": nrm(ks[15], (DEPTH, D, N_EXPERTS), F32) * D ** -0.5,
        "w_gate": nrm(ks[16], (DEPTH, N_EXPERTS, D, EXPERT_FF), F32) * D ** -0.5,
        "w_up": nrm(ks[17], (DEPTH, N_EXPERTS, D, EXPERT_FF), F32) * D ** -0.5,
        "w_down": nrm(ks[18], (DEPTH, N_EXPERTS, EXPERT_FF, D), F32) * EXPERT_FF ** -0.5,
        "norm_final": 1.0 + 0.02 * nrm(ks[19], (D,), F32),
    }


def reference(x, c, ctx, c_ctx, w_ada, b_ada, norm_mix, norm_ffn, w_in, w_out, ret_decay_fwd, ret_decay_bwd,
              conv_w, swa_sink, na_rpb, w_router, w_gate, w_up, w_down, norm_final):
    xc = ctx
    for l in range(DEPTH):
        need_ctx = l < DEPTH - 1
        ada = jax.nn.silu(c) @ w_ada[l] + b_ada[l]
        ada_c = jax.nn.silu(c_ctx) @ w_ada[l] + b_ada[l]
        sh1, sc1, g1, sh2, sc2, g2 = [a[:, None, :] for a in jnp.split(ada, 6, axis=-1)]
        sh1c, sc1c, g1c, sh2c, sc2c, g2c = jnp.split(ada_c, 6, axis=-1)
        h = rmsnorm(x, norm_mix[l]) * (1.0 + sc1) + sh1
        hc = rmsnorm(xc, norm_mix[l]) * (1.0 + sc1c) + sh1c
        y, yc = token_mix(h, hc, w_in[l], w_out[l], ret_decay_fwd[l], ret_decay_bwd[l], conv_w[l],
                          swa_sink[l], na_rpb[l], need_ctx)
        x = x + g1 * y
        h = rmsnorm(x, norm_ffn[l]) * (1.0 + sc2) + sh2
        x = x + g2 * expert_choice_ffn(h, w_router[l], w_gate[l], w_up[l], w_down[l])
        if need_ctx:
            xc = xc + g1c * yc
            hc = rmsnorm(xc, norm_ffn[l]) * (1.0 + sc2c) + sh2c
            xc = xc + g2c * expert_choice_ffn(hc, w_router[l], w_gate[l], w_up[l], w_down[l])
    return rmsnorm(x, norm_final)
```

```python
import functools

import numpy as np
import jax
import jax.numpy as jnp
from jax import lax
from jax.experimental import pallas as pl
from jax.experimental.pallas import tpu as pltpu

D_MODEL = 2048
DEPTH = 2
GRID_W = 64
HEAD_DIM = 64
GROUP_WIDTH = D_MODEL // 4
RET_CHUNK = 128
SWA_HEADS = 8
SWA_KV_HEADS = 2
SWA_WINDOW = 128
SWA_BLOCK = 128
NA_HEADS = 8
NA_ROWS = 8
NA_COLS = 16
N_EXPERTS = 16
EXPERT_FF = D_MODEL // 2
EC_CAPACITY_FACTOR = 2
ROPE_BASE = 10000.0
EPS = 1e-6
NEG_INF = -1e30
F32 = jnp.float32
BF16 = jnp.bfloat16
ADA_ROWS = 16
QK_SCALE = HEAD_DIM ** -0.5

_PIECES = (
    ("ret_q", 0, 512, True, 1.0),
    ("ret_k", 512, 512, True, QK_SCALE),
    ("ret_v", 1024, 512, False, 1.0),
    ("ret_g", 1536, 512, False, 1.0),
    ("conv_b", 2048, 512, False, 1.0),
    ("conv_c", 2560, 512, False, 1.0),
    ("conv_h", 3072, 512, False, 1.0),
    ("swa_q", 3584, 512, True, QK_SCALE),
    ("swa_k", 4096, 128, True, 1.0),
    ("swa_v", 4224, 128, False, 1.0),
    ("na_q", 4352, 512, False, QK_SCALE),
    ("na_k", 4864, 512, False, 1.0),
    ("na_v", 5376, 512, False, 1.0),
)
IN_COLS = 5888
_CTX_KV_ONLY = ("ret_k", "ret_v", "swa_k", "swa_v", "na_k", "na_v")


def _params(sem, vmem_mb):
    return pltpu.CompilerParams(dimension_semantics=sem, vmem_limit_bytes=vmem_mb << 20)


def _silu(x):
    return x * jax.nn.sigmoid(x)


def _dot(a, b):
    return jnp.dot(a, b, preferred_element_type=F32)


def _dot_nt(a, b):
    return lax.dot_general(a, b, (((1,), (1,)), ((), ())), preferred_element_type=F32)


def _dot_tn(a, b):
    return lax.dot_general(a, b, (((0,), (0,)), ((), ())), preferred_element_type=F32)


def _ada_kernel(c_ref, w_ref, b_ref, o_ref):
    s = _silu(c_ref[...]).astype(BF16)
    o_ref[0] = _dot(s, w_ref[0].astype(BF16)) + b_ref[0]


def _ada(cs, w_ada, b_ada):
    depth, d, n = w_ada.shape
    tn = 1024
    return pl.pallas_call(
        _ada_kernel,
        grid=(depth, n // tn),
        in_specs=[
            pl.BlockSpec((ADA_ROWS, d), lambda l, j: (0, 0)),
            pl.BlockSpec((1, d, tn), lambda l, j: (l, 0, j)),
            pl.BlockSpec((1, 1, tn), lambda l, j: (l, 0, j)),
        ],
        out_specs=pl.BlockSpec((1, ADA_ROWS, tn), lambda l, j: (l, 0, j)),
        out_shape=jax.ShapeDtypeStruct((depth, ADA_ROWS, n), F32),
        compiler_params=_params(("arbitrary", "arbitrary"), 40),
        name="ada",
    )(cs, w_ada, b_ada.reshape(depth, 1, n))


def _rope_tables(seq):
    half = HEAD_DIM // 2
    nf = half // 2
    t = np.arange(seq)
    row = (t // GRID_W).astype(np.float32)
    col = (t % GRID_W).astype(np.float32)
    inv = (np.float32(ROPE_BASE) ** (-np.arange(nf, dtype=np.float32) / np.float32(nf))).astype(np.float32)
    ang = np.concatenate([row[:, None] * inv, col[:, None] * inv], axis=-1).astype(np.float32)
    cos, sin = np.cos(ang), np.sin(ang)
    cos64 = np.concatenate([cos, cos], axis=-1)
    sin64 = np.concatenate([-sin, sin], axis=-1)
    return (np.tile(cos64, (1, 2)).astype(np.float32), np.tile(sin64, (1, 2)).astype(np.float32))


def _rope(z, cos, sin_signed):
    w = z.shape[1]
    reps = w // 128
    cos_w = jnp.concatenate([cos] * reps, axis=1) if reps > 1 else cos
    sin_w = jnp.concatenate([sin_signed] * reps, axis=1) if reps > 1 else sin_signed
    lane = lax.broadcasted_iota(jnp.int32, (1, w), 1)
    first_half = (lane & (HEAD_DIM - 1)) < (HEAD_DIM // 2)
    up = pltpu.roll(z, w - HEAD_DIM // 2, 1)
    dn = pltpu.roll(z, HEAD_DIM // 2, 1)
    return z * cos_w + jnp.where(first_half, up, dn) * sin_w


def _rms_modulate(x, gain, scale, shift):
    ms = jnp.mean(x * x, axis=-1, keepdims=True)
    return (x * lax.rsqrt(ms + EPS)) * (gain * (1.0 + scale)) + shift


def _inproj_kernel(*refs, pieces, rope):
    if rope:
        x_ref, gain_ref, sc_ref, sh_ref, cos_ref, sin_ref, w_ref = refs[:7]
        o_refs = refs[7:]
    else:
        x_ref, gain_ref, sc_ref, sh_ref, w_ref = refs[:5]
        o_refs = refs[5:]
    hb = _rms_modulate(x_ref[0], gain_ref[...], sc_ref[0], sh_ref[0]).astype(BF16)
    for (_, off, width, do_rope, scale), o_ref in zip(pieces, o_refs):
        z = _dot(hb, w_ref[:, off:off + width])
        if rope and do_rope:
            z = _rope(z, cos_ref[...], sin_ref[...])
        if scale != 1.0:
            z = z * scale
        o_ref[0] = z.astype(BF16)


def _inproj(x, gain, sc, sh, w_in, *, rope, names, tm):
    bsz, seq, d = x.shape
    pieces = tuple(p for p in _PIECES if p[0] in names)
    vec = pl.BlockSpec((1, 1, d), lambda b, i: (b, 0, 0))
    in_specs = [pl.BlockSpec((1, tm, d), lambda b, i: (b, i, 0)),
                pl.BlockSpec((1, d), lambda b, i: (0, 0)), vec, vec]
    args = [x, gain.reshape(1, d), sc, sh]
    if rope:
        cos, sin = _rope_tables(seq)
        in_specs += [pl.BlockSpec((tm, 128), lambda b, i: (i, 0))] * 2
        args += [jnp.asarray(cos), jnp.asarray(sin)]
    in_specs.append(pl.BlockSpec((d, IN_COLS), lambda b, i: (0, 0), pipeline_mode=pl.Buffered(1)))
    args.append(w_in)
    outs = pl.pallas_call(
        functools.partial(_inproj_kernel, pieces=pieces, rope=rope),
        grid=(bsz, seq // tm),
        in_specs=in_specs,
        out_specs=[pl.BlockSpec((1, tm, p[2]), lambda b, i: (b, i, 0)) for p in pieces],
        out_shape=[jax.ShapeDtypeStruct((bsz, seq, p[2]), BF16) for p in pieces],
        compiler_params=_params(("parallel", "parallel"), 56),
        name="inproj_lat" if rope else "inproj_ctx",
    )(*args)
    return {p[0]: o for p, o in zip(pieces, outs)}


def _log_sigmoid(x):
    return jnp.minimum(x, 0.0) - jnp.log(1.0 + jnp.exp(-jnp.abs(x)))


def _ret_kernel(dec_ref, q_ref, k_ref, v_ref, g_ref, qc_ref, kc_ref, vc_ref, gc_ref, *rest, need_ctx):
    if need_ctx:
        y_ref, yc_ref, sb_ref = rest
    else:
        (y_ref, sb_ref), yc_ref = rest, None
    c = RET_CHUNK
    hd = HEAD_DIM
    n_lat = q_ref.shape[1] // c
    n_ctx = qc_ref.shape[1] // c
    hp = pl.program_id(1)

    ri = lax.broadcasted_iota(jnp.int32, (c, c), 0).astype(F32)
    ci = lax.broadcasted_iota(jnp.int32, (c, c), 1).astype(F32)
    diff = ri - ci
    pos = lax.broadcasted_iota(jnp.int32, (c, hd), 0).astype(F32)

    consts = []
    for j in range(2):
        lg_f = _log_sigmoid(jnp.full((1, c), dec_ref[0, 2 * hp + j], F32))
        lg_b = _log_sigmoid(jnp.full((1, c), dec_ref[1, 2 * hp + j], F32))
        dmat = jnp.where(diff >= 0.0, jnp.exp(lg_f * jnp.maximum(diff, 0.0)),
                         jnp.exp(lg_b * jnp.maximum(-diff, 0.0)))
        lf, lb = lg_f[:, :hd], lg_b[:, :hd]
        consts.append(dict(
            dmat=dmat,
            xi_f=jnp.exp(lf * (pos + 1.0)), xi_b=jnp.exp(lb * (c - pos)),
            ze_f=jnp.exp(lf * (c - 1.0 - pos)), ze_b=jnp.exp(lb * pos),
            cd_f=jnp.exp(lf * float(c)), cd_b=jnp.exp(lb * float(c))))

    def head(ref, r0, j):
        return ref[0, pl.ds(r0, c), :][:, j * hd:(j + 1) * hd]

    def bwd_chunk(k_r, v_r, r0, slot, states):
        new = []
        for j in range(2):
            sb_ref[slot, j] = states[j]
            kz = (head(k_r, r0, j).astype(F32) * consts[j]["ze_b"]).astype(BF16)
            new.append(states[j] * consts[j]["cd_b"] + _dot_tn(kz, head(v_r, r0, j)))
        return tuple(new)

    def fwd_chunk(q_r, k_r, v_r, g_r, o_r, r0, slot, states):
        new, outs = [], []
        for j in range(2):
            cj = consts[j]
            q, k, v = head(q_r, r0, j), head(k_r, r0, j), head(v_r, r0, j)
            if o_r is not None:
                a = (_dot_nt(q, k) * cj["dmat"]).astype(BF16)
                y = (_dot(a, v) + _dot(q, states[j].astype(BF16)) * cj["xi_f"]
                     + _dot(q, sb_ref[slot, j].astype(BF16)) * cj["xi_b"])
                mu = jnp.mean(y, axis=-1, keepdims=True)
                yc = y - mu
                var = jnp.mean(yc * yc, axis=-1, keepdims=True)
                outs.append(_silu(head(g_r, r0, j).astype(F32)) * (yc * lax.rsqrt(var + EPS)))
            kz = (k.astype(F32) * cj["ze_f"]).astype(BF16)
            new.append(states[j] * cj["cd_f"] + _dot_tn(kz, v))
        if o_r is not None:
            o_r[0, pl.ds(r0, c), :] = jnp.concatenate(outs, axis=1).astype(o_r.dtype)
        return tuple(new)

    zero = jnp.zeros((hd, hd), F32)
    st = (zero, zero)
    for cc in range(n_ctx - 1, -1, -1):
        st = bwd_chunk(kc_ref, vc_ref, cc * c, cc, st)

    def bwd_body(i, s):
        cl = n_lat - 1 - i
        return bwd_chunk(k_ref, v_ref, pl.multiple_of(cl * c, c), n_ctx + cl, s)

    lax.fori_loop(0, n_lat, bwd_body, st)

    st = (zero, zero)
    for cc in range(n_ctx):
        st = fwd_chunk(qc_ref, kc_ref, vc_ref, gc_ref, yc_ref, cc * c, cc, st)

    def fwd_body(cl, s):
        return fwd_chunk(q_ref, k_ref, v_ref, g_ref, y_ref, pl.multiple_of(cl * c, c), n_ctx + cl, s)

    lax.fori_loop(0, n_lat, fwd_body, st)


def _retention(dec, p, pc, need_ctx):
    bsz, seq, w = p["ret_q"].shape
    t = pc["ret_k"].shape[1]
    n_pairs = w // 128
    lat = pl.BlockSpec((1, seq, 128), lambda b, h: (b, 0, h))
    ctx = pl.BlockSpec((1, t, 128), lambda b, h: (b, 0, h))
    out_specs, out_shape = [lat], [jax.ShapeDtypeStruct((bsz, seq, w), BF16)]
    if need_ctx:
        out_specs.append(ctx)
        out_shape.append(jax.ShapeDtypeStruct((bsz, t, w), BF16))
        qc, gc = pc["ret_q"], pc["ret_g"]
    else:
        qc, gc = pc["ret_k"], pc["ret_k"]
    n_chunks = (seq + t) // RET_CHUNK
    outs = pl.pallas_call(
        functools.partial(_ret_kernel, need_ctx=need_ctx),
        grid=(bsz, n_pairs),
        in_specs=[pl.BlockSpec(memory_space=pltpu.SMEM), lat, lat, lat, lat, ctx, ctx, ctx, ctx],
        out_specs=out_specs,
        out_shape=out_shape,
        scratch_shapes=[pltpu.VMEM((n_chunks, 2, HEAD_DIM, HEAD_DIM), F32)],
        compiler_params=_params(("parallel", "parallel"), 32),
        name="retention",
    )(dec, p["ret_q"], p["ret_k"], p["ret_v"], p["ret_g"], qc, pc["ret_k"], pc["ret_v"], gc)
    return (outs[0], outs[1]) if need_ctx else (outs[0], None)


def _conv_kernel(w_ref, b_ref, c_ref, h_ref, o_ref):
    seq = b_ref.shape[1]
    rows = min(256, seq)
    halo = 16
    w0, w1, w2 = w_ref[0:1, :], w_ref[1:2, :], w_ref[2:3, :]
    for r0 in range(0, seq, rows):
        lo, hi = max(0, r0 - halo), min(seq, r0 + rows + halo)
        u = c_ref[0, lo:hi, :].astype(F32) * h_ref[0, lo:hi, :].astype(F32)
        n = hi - lo
        t = lo + lax.broadcasted_iota(jnp.int32, (n, 1), 0)
        prev = jnp.where(t == 0, 0.0, pltpu.roll(u, 1, 0))
        nxt = jnp.where(t == seq - 1, 0.0, pltpu.roll(u, n - 1, 0))
        y = prev * w0 + u * w1 + nxt * w2
        y = y[r0 - lo:r0 - lo + rows]
        o_ref[0, r0:r0 + rows, :] = (b_ref[0, r0:r0 + rows, :].astype(F32) * y).astype(o_ref.dtype)


def _short_conv(conv_w, b, c, h):
    bsz, seq, w = b.shape
    blk = pl.BlockSpec((1, seq, w), lambda i: (i, 0, 0))
    return pl.pallas_call(
        _conv_kernel,
        grid=(bsz,),
        in_specs=[pl.BlockSpec((3, w), lambda i: (0, 0)), blk, blk, blk],
        out_specs=blk,
        out_shape=jax.ShapeDtypeStruct((bsz, seq, w), BF16),
        compiler_params=_params(("parallel",), 48),
        name="short_conv",
    )(conv_w, b, c, h)


def _softmax_pv(scores, values, extra=None):
    m = functools.reduce(jnp.maximum, [jnp.max(s, axis=-1, keepdims=True) for s in scores])
    if extra is not None:
        m = jnp.maximum(m, extra)
    den = jnp.exp(extra - m) if extra is not None else 0.0
    acc = 0.0
    for s, v in zip(scores, values):
        p = jnp.exp(s - m)
        den = den + jnp.sum(p, axis=-1, keepdims=True)
        acc = acc + _dot(p.astype(BF16), v)
    return acc / den


def _swa_kernel(sink_ref, q_ref, k_ref, v_ref, kc_ref, vc_ref, o_ref):
    blk = SWA_BLOCK
    win = 3 * blk
    seq = k_ref.shape[1]
    grp = SWA_HEADS // SWA_KV_HEADS
    hd = HEAD_DIM
    i = pl.program_id(1)
    start = pl.multiple_of(jnp.clip((i - 1) * blk, 0, seq - win), blk)
    q = q_ref[0]
    kw = k_ref[0, pl.ds(start, win), :]
    vw = v_ref[0, pl.ds(start, win), :]
    kc, vc = kc_ref[0], vc_ref[0]
    qpos = i * blk + lax.broadcasted_iota(jnp.int32, (blk, win), 0)
    kpos = start + lax.broadcasted_iota(jnp.int32, (blk, win), 1)
    valid = jnp.abs(kpos - qpos) <= SWA_WINDOW
    valid = jnp.concatenate([valid] * grp, axis=0)
    outs = [None] * SWA_HEADS
    for j in range(SWA_KV_HEADS):
        heads = range(j * grp, (j + 1) * grp)
        qg = jnp.concatenate([q[:, h * hd:(h + 1) * hd] for h in heads], axis=0)
        sl = slice(j * hd, (j + 1) * hd)
        s_win = jnp.where(valid, _dot_nt(qg, kw[:, sl]), NEG_INF)
        s_ctx = _dot_nt(qg, kc[:, sl])
        sink = jnp.concatenate([jnp.full((blk, 1), sink_ref[h], F32) for h in heads], axis=0)
        o = _softmax_pv([s_win, s_ctx], [vw[:, sl], vc[:, sl]], sink)
        for g, h in enumerate(heads):
            outs[h] = o[g * blk:(g + 1) * blk]
    o_ref[0] = jnp.concatenate(outs, axis=1).astype(o_ref.dtype)


def _swa(sink, p, pc):
    bsz, seq, w = p["swa_q"].shape
    t = pc["swa_k"].shape[1]
    kvw = SWA_KV_HEADS * HEAD_DIM
    kv = pl.BlockSpec((1, seq, kvw), lambda b, i: (b, 0, 0))
    kvc = pl.BlockSpec((1, t, kvw), lambda b, i: (b, 0, 0))
    qb = pl.BlockSpec((1, SWA_BLOCK, w), lambda b, i: (b, i, 0))
    return pl.pallas_call(
        _swa_kernel,
        grid=(bsz, seq // SWA_BLOCK),
        in_specs=[pl.BlockSpec(memory_space=pltpu.SMEM), qb, kv, kv, kvc, kvc],
        out_specs=qb,
        out_shape=jax.ShapeDtypeStruct((bsz, seq, w), BF16),
        compiler_params=_params(("parallel", "parallel"), 32),
        name="swa",
    )(sink, p["swa_q"], p["swa_k"], p["swa_v"], pc["swa_k"], pc["swa_v"])


_NA_CFG_ROWS = (0, 1, 2, 3, 4, 29, 30, 31)


def _na_bias_table(rpb, rows):
    wr, wc = NA_ROWS, NA_COLS
    r = np.array(_NA_CFG_ROWS)
    row_idx = np.clip(r - wr // 2, 0, rows - wr)[:, None] + np.arange(wr)[None, :]
    dr = row_idx - r[:, None] + (NA_ROWS - 1)
    cq = np.arange(GRID_W)
    col_start = np.clip(cq - wc // 2, 0, GRID_W - wc)
    col_ok = (cq[None, :] >= col_start[:, None]) & (cq[None, :] < col_start[:, None] + wc)
    dc = np.clip(cq[None, :] - cq[:, None], -(wc - 1), wc - 1) + (NA_COLS - 1)
    bias = rpb.astype(F32)[:, dr[:, :, None, None], dc[None, None, :, :]]
    bias = jnp.where(col_ok[None, None, None], bias, NEG_INF)
    return bias.transpose(1, 0, 3, 2, 4).reshape(len(_NA_CFG_ROWS), NA_HEADS, GRID_W, wr * GRID_W)


def _na_kernel(q_ref, k_ref, v_ref, kc_ref, vc_ref, bias_ref, o_ref):
    rows = k_ref.shape[1] // GRID_W
    nk = NA_ROWS * GRID_W
    hd = HEAD_DIM
    r = pl.program_id(1)
    st = pl.multiple_of(jnp.clip(r - NA_ROWS // 2, 0, rows - NA_ROWS) * GRID_W, GRID_W)
    q = q_ref[0]
    kb = k_ref[0, pl.ds(st, nk), :]
    vb = v_ref[0, pl.ds(st, nk), :]
    kc, vc = kc_ref[0], vc_ref[0]
    outs = []
    for h in range(NA_HEADS):
        sl = slice(h * hd, (h + 1) * hd)
        s_nb = _dot_nt(q[:, sl], kb[:, sl]) + bias_ref[0, h]
        s_ctx = _dot_nt(q[:, sl], kc[:, sl])
        outs.append(_softmax_pv([s_nb, s_ctx], [vb[:, sl], vc[:, sl]]))
    o_ref[0] = jnp.concatenate(outs, axis=1).astype(o_ref.dtype)


def _na(bias, p, pc):
    bsz, seq, w = p["na_q"].shape
    t = pc["na_k"].shape[1]
    rows = seq // GRID_W
    kv = pl.BlockSpec((1, seq, w), lambda b, r: (b, 0, 0))
    kvc = pl.BlockSpec((1, t, w), lambda b, r: (b, 0, 0))
    qb = pl.BlockSpec((1, GRID_W, w), lambda b, r: (b, r, 0))

    def cfg(b, r):
        return (jnp.where(r < 4, r, jnp.where(r <= rows - 4, 4, r - (rows - 8))), 0, 0, 0)

    return pl.pallas_call(
        _na_kernel,
        grid=(bsz, rows),
        in_specs=[qb, kv, kv, kvc, kvc, pl.BlockSpec((1, NA_HEADS, GRID_W, NA_ROWS * GRID_W), cfg)],
        out_specs=qb,
        out_shape=jax.ShapeDtypeStruct((bsz, seq, w), BF16),
        compiler_params=_params(("parallel", "arbitrary"), 32),
        name="na",
    )(p["na_q"], p["na_k"], p["na_v"], pc["na_k"], pc["na_v"], bias)


def _ctx_attn_kernel(sink_ref, sq_ref, sk_ref, sv_ref, nq_ref, nk_ref, nv_ref, so_ref, no_ref):
    t = sq_ref.shape[1]
    hd = HEAD_DIM
    grp = SWA_HEADS // SWA_KV_HEADS
    q, k, v = sq_ref[0], sk_ref[0], sv_ref[0]
    outs = [None] * SWA_HEADS
    for j in range(SWA_KV_HEADS):
        heads = range(j * grp, (j + 1) * grp)
        qg = jnp.concatenate([q[:, h * hd:(h + 1) * hd] for h in heads], axis=0)
        sl = slice(j * hd, (j + 1) * hd)
        sink = jnp.concatenate([jnp.full((t, 1), sink_ref[h], F32) for h in heads], axis=0)
        o = _softmax_pv([_dot_nt(qg, k[:, sl])], [v[:, sl]], sink)
        for g, h in enumerate(heads):
            outs[h] = o[g * t:(g + 1) * t]
    so_ref[0] = jnp.concatenate(outs, axis=1).astype(so_ref.dtype)
    q, k, v = nq_ref[0], nk_ref[0], nv_ref[0]
    outs = []
    for h in range(NA_HEADS):
        sl = slice(h * hd, (h + 1) * hd)
        outs.append(_softmax_pv([_dot_nt(q[:, sl], k[:, sl])], [v[:, sl]]))
    no_ref[0] = jnp.concatenate(outs, axis=1).astype(no_ref.dtype)


def _ctx_attn(sink, pc):
    bsz, t, w = pc["swa_q"].shape
    kvw = SWA_KV_HEADS * HEAD_DIM
    full = pl.BlockSpec((1, t, w), lambda b: (b, 0, 0))
    kv = pl.BlockSpec((1, t, kvw), lambda b: (b, 0, 0))
    return pl.pallas_call(
        _ctx_attn_kernel,
        grid=(bsz,),
        in_specs=[pl.BlockSpec(memory_space=pltpu.SMEM), full, kv, kv, full, full, full],
        out_specs=[full, full],
        out_shape=[jax.ShapeDtypeStruct((bsz, t, w), BF16)] * 2,
        compiler_params=_params(("parallel",), 32),
        name="ctx_attn",
    )(sink, pc["swa_q"], pc["swa_k"], pc["swa_v"], pc["na_q"], pc["na_k"], pc["na_v"])


def _outproj_kernel(yr_ref, yc_ref, ys_ref, yn_ref, w_ref, x_ref, g1_ref, gain_ref, sc_ref, sh_ref, wr_ref,
                    xo_ref, h_ref, lg_ref):
    gw = GROUP_WIDTH
    acc = _dot(yr_ref[0], w_ref[0:gw, :])
    acc += _dot(yc_ref[0], w_ref[gw:2 * gw, :])
    acc += _dot(ys_ref[0], w_ref[2 * gw:3 * gw, :])
    acc += _dot(yn_ref[0], w_ref[3 * gw:4 * gw, :])
    xn = x_ref[0] + g1_ref[0] * acc
    xo_ref[0] = xn
    h = _rms_modulate(xn, gain_ref[...], sc_ref[0], sh_ref[0])
    h_ref[0] = h.astype(BF16)
    lg_ref[0] = jnp.dot(h, wr_ref[...], preferred_element_type=F32, precision=lax.Precision.HIGHEST)


def _outproj(ys, w_out, x, g1, gain, sc, sh, w_router, *, tm):
    bsz, seq, d = x.shape
    yb = pl.BlockSpec((1, tm, GROUP_WIDTH), lambda b, i: (b, i, 0))
    xb = pl.BlockSpec((1, tm, d), lambda b, i: (b, i, 0))
    vec = pl.BlockSpec((1, 1, d), lambda b, i: (b, 0, 0))
    return pl.pallas_call(
        _outproj_kernel,
        grid=(bsz, seq // tm),
        in_specs=[yb, yb, yb, yb,
                  pl.BlockSpec((d, d), lambda b, i: (0, 0), pipeline_mode=pl.Buffered(1)),
                  xb, vec, pl.BlockSpec((1, d), lambda b, i: (0, 0)), vec, vec,
                  pl.BlockSpec((d, N_EXPERTS), lambda b, i: (0, 0))],
        out_specs=[xb, xb, pl.BlockSpec((1, tm, N_EXPERTS), lambda b, i: (b, i, 0))],
        out_shape=[jax.ShapeDtypeStruct((bsz, seq, d), F32), jax.ShapeDtypeStruct((bsz, seq, d), BF16),
                   jax.ShapeDtypeStruct((bsz, seq, N_EXPERTS), F32)],
        compiler_params=_params(("parallel", "parallel"), 48),
        name="outproj",
    )(*ys, w_out, x, g1, gain.reshape(1, d), sc, sh, w_router)


def _cumsum_lanes(x, tri):
    outs = []
    carry = jnp.zeros((x.shape[0], 1), F32)
    for c0 in range(0, x.shape[1], 128):
        cs = _dot(x[:, c0:c0 + 128], tri) + carry
        outs.append(cs)
        carry = cs[:, 127:128]
    return jnp.concatenate(outs, axis=1) if len(outs) > 1 else outs[0]


def _router_kernel(lg_ref, pos_ref, gate_ref, *, cap):
    lg = lg_ref[0]
    e = jnp.exp(lg - jnp.max(lg, axis=0, keepdims=True))
    aff = e / jnp.sum(e, axis=0, keepdims=True)
    capf = float(cap)
    lo = jnp.zeros((lg.shape[0], 1), jnp.int32)
    for bit in range(30, -1, -1):
        cand = lo | (1 << bit)
        cnt = jnp.sum(jnp.where(aff >= pltpu.bitcast(cand, F32), 1.0, 0.0), axis=1, keepdims=True)
        lo = jnp.where(cnt >= capf, cand, lo)
    kth = pltpu.bitcast(lo, F32)
    gt = aff > kth
    eq = aff == kth
    need = capf - jnp.sum(jnp.where(gt, 1.0, 0.0), axis=1, keepdims=True)
    ri = lax.broadcasted_iota(jnp.int32, (128, 128), 0)
    ci = lax.broadcasted_iota(jnp.int32, (128, 128), 1)
    tri = jnp.where(ri <= ci, 1.0, 0.0).astype(BF16)
    eq_rank = _cumsum_lanes(jnp.where(eq, 1.0, 0.0).astype(BF16), tri)
    sel = gt | (eq & (eq_rank <= need))
    slot = _cumsum_lanes(jnp.where(sel, 1.0, 0.0).astype(BF16), tri) - 1.0
    pos_ref[0] = jnp.where(sel, slot, -1.0).astype(jnp.int32)
    gate_ref[0] = aff


def _router(logits_t, cap):
    bsz, n_e, t = logits_t.shape
    blk = pl.BlockSpec((1, n_e, t), lambda b: (b, 0, 0))
    return pl.pallas_call(
        functools.partial(_router_kernel, cap=cap),
        grid=(bsz,),
        in_specs=[blk],
        out_specs=[blk, blk],
        out_shape=[jax.ShapeDtypeStruct((bsz, n_e, t), jnp.int32), jax.ShapeDtypeStruct((bsz, n_e, t), F32)],
        compiler_params=_params(("parallel",), 32),
        name="router",
    )(logits_t)


def _expert_kernel(pos_ref, gate_ref, h_ref, wg_ref, wu_ref, wd_ref, o_ref, *, cap):
    t = h_ref.shape[1]
    hit = lax.broadcasted_iota(jnp.int32, (cap, t), 0) == pos_ref[0, 0]
    xs = _dot(jnp.where(hit, 1.0, 0.0).astype(BF16), h_ref[0]).astype(BF16)
    a = _dot(xs, wg_ref[0])
    u = _dot(xs, wu_ref[0])
    ye = _dot((_silu(a) * u).astype(BF16), wd_ref[0])
    gate = jnp.sum(jnp.where(hit, gate_ref[0, 0], 0.0), axis=1, keepdims=True)
    o_ref[0, 0] = (ye * gate).astype(o_ref.dtype)


def _experts(pos, gate, h, wg, wu, wd, cap):
    bsz, t, d = h.shape
    n_e, _, ff = wg.shape
    row = pl.BlockSpec((1, 1, 1, t), lambda e, b: (b, e, 0, 0))
    return pl.pallas_call(
        functools.partial(_expert_kernel, cap=cap),
        grid=(n_e, bsz),
        in_specs=[row, row, pl.BlockSpec((1, t, d), lambda e, b: (b, 0, 0)),
                  pl.BlockSpec((1, d, ff), lambda e, b: (e, 0, 0)),
                  pl.BlockSpec((1, d, ff), lambda e, b: (e, 0, 0)),
                  pl.BlockSpec((1, ff, d), lambda e, b: (e, 0, 0))],
        out_specs=pl.BlockSpec((1, 1, cap, d), lambda e, b: (e, b, 0, 0)),
        out_shape=jax.ShapeDtypeStruct((n_e, bsz, cap, d), BF16),
        compiler_params=_params(("arbitrary", "arbitrary"), 56),
        name="experts",
    )(pos.reshape(bsz, n_e, 1, t), gate.reshape(bsz, n_e, 1, t), h, wg, wu, wd)


def _combine_kernel(pos_ref, ye_ref, x_ref, g2_ref, o_ref, *, cap):
    e = pl.program_id(2)
    tb = x_ref.shape[1]
    hit = lax.broadcasted_iota(jnp.int32, (cap, tb), 0) == pos_ref[0, 0]
    contrib = _dot_tn(jnp.where(hit, 1.0, 0.0).astype(BF16), ye_ref[0, 0])

    @pl.when(e == 0)
    def _():
        o_ref[0] = contrib

    @pl.when(e > 0)
    def _():
        o_ref[0] += contrib

    @pl.when(e == pl.num_programs(2) - 1)
    def _():
        o_ref[0] = x_ref[0] + g2_ref[0] * o_ref[0]


def _combine(pos, ye, x, g2, cap, tb):
    bsz, t, d = x.shape
    n_e = ye.shape[0]
    xb = pl.BlockSpec((1, tb, d), lambda b, i, e: (b, i, 0))
    return pl.pallas_call(
        functools.partial(_combine_kernel, cap=cap),
        grid=(bsz, t // tb, n_e),
        in_specs=[pl.BlockSpec((1, 1, 1, tb), lambda b, i, e: (b, e, 0, i)),
                  pl.BlockSpec((1, 1, cap, d), lambda b, i, e: (e, b, 0, 0)),
                  xb, pl.BlockSpec((1, 1, d), lambda b, i, e: (b, 0, 0))],
        out_specs=xb,
        out_shape=jax.ShapeDtypeStruct((bsz, t, d), F32),
        compiler_params=_params(("parallel", "parallel", "arbitrary"), 56),
        name="combine",
    )(pos.reshape(bsz, n_e, 1, t), ye, x, g2)


def _moe(x, h, logits, g2, wg, wu, wd):
    bsz, t, d = x.shape
    cap = EC_CAPACITY_FACTOR * t // N_EXPERTS
    pos, gate = _router(jnp.swapaxes(logits, 1, 2), cap)
    ye = _experts(pos, gate, h, wg, wu, wd, cap)
    return _combine(pos, ye, x, g2, cap, min(t, 1024))


def _final_norm_kernel(x_ref, g_ref, o_ref):
    x = x_ref[0]
    ms = jnp.mean(x * x, axis=-1, keepdims=True)
    o_ref[0] = (x * lax.rsqrt(ms + EPS)) * g_ref[...]


def _final_norm(x, gain, tm):
    bsz, seq, d = x.shape
    xb = pl.BlockSpec((1, tm, d), lambda b, i: (b, i, 0))
    return pl.pallas_call(
        _final_norm_kernel,
        grid=(bsz, seq // tm),
        in_specs=[xb, pl.BlockSpec((1, d), lambda b, i: (0, 0))],
        out_specs=xb,
        out_shape=jax.ShapeDtypeStruct((bsz, seq, d), F32),
        compiler_params=_params(("parallel", "parallel"), 32),
        name="final_norm",
    )(x, gain.reshape(1, d))


def _layer(x, xc, mods, mods_c, lw, need_ctx):
    sh1, sc1, g1, sh2, sc2, g2 = mods
    sh1c, sc1c, g1c, sh2c, sc2c, g2c = mods_c
    names_lat = tuple(p[0] for p in _PIECES)
    p = _inproj(x, lw["norm_mix"], sc1, sh1, lw["w_in"], rope=True, names=names_lat, tm=256)
    pc = _inproj(xc, lw["norm_mix"], sc1c, sh1c, lw["w_in"], rope=False,
                 names=names_lat if need_ctx else _CTX_KV_ONLY, tm=256)

    y_ret, yc_ret = _retention(lw["ret_decay"], p, pc, need_ctx)
    y_conv = _short_conv(lw["conv_w"], p["conv_b"], p["conv_c"], p["conv_h"])
    y_swa = _swa(lw["swa_sink"], p, pc)
    y_na = _na(_na_bias_table(lw["na_rpb"], x.shape[1] // GRID_W), p, pc)
    x, h, logits = _outproj((y_ret, y_conv, y_swa, y_na), lw["w_out"], x, g1, lw["norm_ffn"], sc2, sh2,
                            lw["w_router"], tm=256)
    x = _moe(x, h, logits, g2, lw["w_gate"], lw["w_up"], lw["w_down"])
    if need_ctx:
        yc_conv = _short_conv(lw["conv_w"], pc["conv_b"], pc["conv_c"], pc["conv_h"])
        yc_swa, yc_na = _ctx_attn(lw["swa_sink"], pc)
        xc, hc, logits_c = _outproj((yc_ret, yc_conv, yc_swa, yc_na), lw["w_out"], xc, g1c, lw["norm_ffn"],
                                    sc2c, sh2c, lw["w_router"], tm=256)
        xc = _moe(xc, hc, logits_c, g2c, lw["w_gate"], lw["w_up"], lw["w_down"])
    return x, xc


def kernel(x, c, ctx, c_ctx, w_ada, b_ada, norm_mix, norm_ffn, w_in, w_out, ret_decay_fwd, ret_decay_bwd,
           conv_w, swa_sink, na_rpb, w_router, w_gate, w_up, w_down, norm_final):
    bsz, _, d = x.shape
    depth = w_ada.shape[0]
    assert bsz + 1 <= ADA_ROWS
    cs = jnp.zeros((ADA_ROWS, d), F32).at[:bsz].set(c).at[bsz].set(c_ctx)
    ada = _ada(cs, w_ada, b_ada)
    w_in_b, w_out_b = w_in.astype(BF16), w_out.astype(BF16)
    wg_b, wu_b, wd_b = w_gate.astype(BF16), w_up.astype(BF16), w_down.astype(BF16)
    xc = ctx
    for l in range(depth):
        mods = tuple(ada[l, :bsz, i * d:(i + 1) * d].reshape(bsz, 1, d) for i in range(6))
        mods_c = tuple(jnp.broadcast_to(ada[l, bsz, i * d:(i + 1) * d].reshape(1, 1, d), (bsz, 1, d))
                       for i in range(6))
        lw = dict(norm_mix=norm_mix[l], norm_ffn=norm_ffn[l], w_in=w_in_b[l], w_out=w_out_b[l],
                  ret_decay=jnp.stack([ret_decay_fwd[l], ret_decay_bwd[l]]), conv_w=conv_w[l],
                  swa_sink=swa_sink[l], na_rpb=na_rpb[l], w_router=w_router[l],
                  w_gate=wg_b[l], w_up=wu_b[l], w_down=wd_b[l])
        x, xc = _layer(x, xc, mods, mods_c, lw, need_ctx=l < depth - 1)
    return _final_norm(x, norm_final, 256)
```

```python
import functools

import numpy as np
import jax
import jax.numpy as jnp
from jax import lax
from jax.experimental import pallas as pl
from jax.experimental.pallas import tpu as pltpu

D_MODEL = 2048
DEPTH = 2
GRID_W = 64
HEAD_DIM = 64
GROUP_WIDTH = D_MODEL // 4
RET_CHUNK = 128
SWA_HEADS = 8
SWA_KV_HEADS = 2
SWA_WINDOW = 128
SWA_BLOCK = 128
NA_HEADS = 8
NA_ROWS = 8
NA_COLS = 16
N_EXPERTS = 16
EXPERT_FF = D_MODEL // 2
EC_CAPACITY_FACTOR = 2
ROPE_BASE = 10000.0
EPS = 1e-6
NEG_INF = -1e30
F32 = jnp.float32
BF16 = jnp.bfloat16
ADA_ROWS = 16
QK_SCALE = HEAD_DIM ** -0.5
PAIR = 2 * HEAD_DIM

_PIECES = (
    ("ret_q", 0, 512, True, 1.0),
    ("ret_k", 512, 512, True, QK_SCALE),
    ("ret_v", 1024, 512, False, 1.0),
    ("ret_g", 1536, 512, False, 1.0),
    ("conv_b", 2048, 512, False, 1.0),
    ("conv_c", 2560, 512, False, 1.0),
    ("conv_h", 3072, 512, False, 1.0),
    ("swa_q", 3584, 512, True, QK_SCALE),
    ("swa_k", 4096, 256, True, 1.0),
    ("swa_v", 4352, 256, False, 1.0),
    ("na_q", 4608, 512, False, QK_SCALE),
    ("na_k", 5120, 512, False, 1.0),
    ("na_v", 5632, 512, False, 1.0),
)
IN_COLS = 6144
_SWA_KV_COL = 4096
_CTX_KV_ONLY = ("ret_k", "ret_v", "swa_k", "swa_v", "na_k", "na_v")


def _widen_swa_kv(w):
    hd, c0 = HEAD_DIM, _SWA_KV_COL
    parts = [w[:, :c0]]
    for base in (c0, c0 + SWA_KV_HEADS * hd):
        for j in range(SWA_KV_HEADS):
            head = w[:, base + j * hd:base + (j + 1) * hd]
            parts += [head, head]
    parts.append(w[:, c0 + 2 * SWA_KV_HEADS * hd:])
    return jnp.concatenate(parts, axis=1)


def _params(sem, vmem_mb):
    return pltpu.CompilerParams(dimension_semantics=sem, vmem_limit_bytes=vmem_mb << 20)


def _silu(x):
    return x * jax.nn.sigmoid(x)


def _dot(a, b):
    return jnp.dot(a, b, preferred_element_type=F32)


def _dot_nt(a, b):
    return lax.dot_general(a, b, (((1,), (1,)), ((), ())), preferred_element_type=F32)


def _dot_tn(a, b):
    return lax.dot_general(a, b, (((0,), (0,)), ((), ())), preferred_element_type=F32)


def _first_head_lanes():
    return lax.broadcasted_iota(jnp.int32, (1, PAIR), 1) < HEAD_DIM


def _keep_head(x, lo, j):
    return jnp.where(lo if j == 0 else jnp.logical_not(lo), x, jnp.zeros_like(x))


def _ada_kernel(c_ref, w_ref, b_ref, o_ref):
    s = _silu(c_ref[...]).astype(BF16)
    o_ref[0] = _dot(s, w_ref[0].astype(BF16)) + b_ref[0]


def _ada(cs, w_ada, b_ada):
    depth, d, n = w_ada.shape
    tn = 1024
    return pl.pallas_call(
        _ada_kernel,
        grid=(depth, n // tn),
        in_specs=[
            pl.BlockSpec((ADA_ROWS, d), lambda l, j: (0, 0)),
            pl.BlockSpec((1, d, tn), lambda l, j: (l, 0, j)),
            pl.BlockSpec((1, 1, tn), lambda l, j: (l, 0, j)),
        ],
        out_specs=pl.BlockSpec((1, ADA_ROWS, tn), lambda l, j: (l, 0, j)),
        out_shape=jax.ShapeDtypeStruct((depth, ADA_ROWS, n), F32),
        compiler_params=_params(("arbitrary", "arbitrary"), 40),
        name="ada",
    )(cs, w_ada, b_ada.reshape(depth, 1, n))


def _rope_tables(seq):
    half = HEAD_DIM // 2
    nf = half // 2
    t = np.arange(seq)
    row = (t // GRID_W).astype(np.float32)
    col = (t % GRID_W).astype(np.float32)
    inv = (np.float32(ROPE_BASE) ** (-np.arange(nf, dtype=np.float32) / np.float32(nf))).astype(np.float32)
    ang = np.concatenate([row[:, None] * inv, col[:, None] * inv], axis=-1).astype(np.float32)
    cos, sin = np.cos(ang), np.sin(ang)
    cos64 = np.concatenate([cos, cos], axis=-1)
    sin64 = np.concatenate([-sin, sin], axis=-1)
    return (np.tile(cos64, (1, 2)).astype(np.float32), np.tile(sin64, (1, 2)).astype(np.float32))


def _rope(z, cos, sin_signed):
    w = z.shape[1]
    reps = w // 128
    cos_w = jnp.concatenate([cos] * reps, axis=1) if reps > 1 else cos
    sin_w = jnp.concatenate([sin_signed] * reps, axis=1) if reps > 1 else sin_signed
    lane = lax.broadcasted_iota(jnp.int32, (1, w), 1)
    first_half = (lane & (HEAD_DIM - 1)) < (HEAD_DIM // 2)
    up = pltpu.roll(z, w - HEAD_DIM // 2, 1)
    dn = pltpu.roll(z, HEAD_DIM // 2, 1)
    return z * cos_w + jnp.where(first_half, up, dn) * sin_w


def _rms_modulate(x, gain, scale, shift):
    ms = jnp.mean(x * x, axis=-1, keepdims=True)
    return (x * lax.rsqrt(ms + EPS)) * (gain * (1.0 + scale)) + shift


def _inproj_kernel(*refs, pieces, rope):
    if rope:
        x_ref, gain_ref, sc_ref, sh_ref, cos_ref, sin_ref, w_ref = refs[:7]
        o_refs = refs[7:]
    else:
        x_ref, gain_ref, sc_ref, sh_ref, w_ref = refs[:5]
        o_refs = refs[5:]
    hb = _rms_modulate(x_ref[0], gain_ref[...], sc_ref[0], sh_ref[0]).astype(BF16)
    for (_, off, width, do_rope, scale), o_ref in zip(pieces, o_refs):
        z = _dot(hb, w_ref[:, off:off + width])
        if rope and do_rope:
            z = _rope(z, cos_ref[...], sin_ref[...])
        if scale != 1.0:
            z = z * scale
        o_ref[0] = z.astype(BF16)


def _inproj(x, gain, sc, sh, w_in, *, rope, names, tm):
    bsz, seq, d = x.shape
    pieces = tuple(p for p in _PIECES if p[0] in names)
    vec = pl.BlockSpec((1, 1, d), lambda b, i: (b, 0, 0))
    in_specs = [pl.BlockSpec((1, tm, d), lambda b, i: (b, i, 0)),
                pl.BlockSpec((1, d), lambda b, i: (0, 0)), vec, vec]
    args = [x, gain.reshape(1, d), sc, sh]
    if rope:
        cos, sin = _rope_tables(seq)
        in_specs += [pl.BlockSpec((tm, 128), lambda b, i: (i, 0))] * 2
        args += [jnp.asarray(cos), jnp.asarray(sin)]
    in_specs.append(pl.BlockSpec((d, IN_COLS), lambda b, i: (0, 0), pipeline_mode=pl.Buffered(1)))
    args.append(w_in)
    outs = pl.pallas_call(
        functools.partial(_inproj_kernel, pieces=pieces, rope=rope),
        grid=(bsz, seq // tm),
        in_specs=in_specs,
        out_specs=[pl.BlockSpec((1, tm, p[2]), lambda b, i: (b, i, 0)) for p in pieces],
        out_shape=[jax.ShapeDtypeStruct((bsz, seq, p[2]), BF16) for p in pieces],
        compiler_params=_params(("parallel", "parallel"), 56),
        name="inproj_lat" if rope else "inproj_ctx",
    )(*args)
    return {p[0]: o for p, o in zip(pieces, outs)}


def _log_sigmoid(x):
    return jnp.minimum(x, 0.0) - jnp.log(1.0 + jnp.exp(-jnp.abs(x)))


def _ret_kernel(dec_ref, q_ref, k_ref, v_ref, g_ref, qc_ref, kc_ref, vc_ref, gc_ref, *rest, need_ctx):
    if need_ctx:
        y_ref, yc_ref, sb_ref = rest
    else:
        (y_ref, sb_ref), yc_ref = rest, None
    c = RET_CHUNK
    hd = HEAD_DIM
    n_lat = q_ref.shape[1] // c
    n_ctx = qc_ref.shape[1] // c
    hp = pl.program_id(1)

    lo = _first_head_lanes()
    ri = lax.broadcasted_iota(jnp.int32, (c, c), 0)
    ci = lax.broadcasted_iota(jnp.int32, (c, c), 1)
    diff = (ri - ci).astype(F32)
    same_head = (ri < hd) == (ci < hd)
    pos = lax.broadcasted_iota(jnp.int32, (c, PAIR), 0).astype(F32)

    lg_f = [_log_sigmoid(jnp.full((1, c), dec_ref[0, 2 * hp + j], F32)) for j in range(2)]
    lg_b = [_log_sigmoid(jnp.full((1, c), dec_ref[1, 2 * hp + j], F32)) for j in range(2)]
    dmat = [jnp.where(diff >= 0.0, jnp.exp(lg_f[j] * jnp.maximum(diff, 0.0)),
                      jnp.exp(lg_b[j] * jnp.maximum(-diff, 0.0))) for j in range(2)]
    lf = jnp.where(lo, lg_f[0], lg_f[1])
    lb = jnp.where(lo, lg_b[0], lg_b[1])
    xi_f, xi_b = jnp.exp(lf * (pos + 1.0)), jnp.exp(lb * (c - pos))
    ze_f, ze_b = jnp.exp(lf * (c - 1.0 - pos)), jnp.exp(lb * pos)
    cd_f, cd_b = jnp.exp(lf * float(c)), jnp.exp(lb * float(c))

    def rows(ref, r0):
        return ref[0, pl.ds(r0, c), :]

    def kv_outer(k, v, zeta):
        kz = (k.astype(F32) * zeta).astype(BF16)
        return jnp.where(same_head, _dot_tn(kz, v), 0.0)

    def half_sum(x):
        s_lo = jnp.sum(jnp.where(lo, x, 0.0), axis=-1, keepdims=True)
        s_all = jnp.sum(x, axis=-1, keepdims=True)
        return jnp.where(lo, s_lo, s_all - s_lo)

    def bwd_chunk(k_r, v_r, r0, slot, state):
        sb_ref[slot] = state
        return state * cd_b + kv_outer(rows(k_r, r0), rows(v_r, r0), ze_b)

    def fwd_chunk(q_r, k_r, v_r, g_r, o_r, r0, slot, state):
        k, v = rows(k_r, r0), rows(v_r, r0)
        if o_r is not None:
            q = rows(q_r, r0)
            intra = []
            for j in range(2):
                a = (_dot_nt(_keep_head(q, lo, j), k) * dmat[j]).astype(BF16)
                intra.append(_dot(a, v))
            y = (jnp.where(lo, intra[0], intra[1]) + _dot(q, state.astype(BF16)) * xi_f
                 + _dot(q, sb_ref[slot].astype(BF16)) * xi_b)
            yc = y - half_sum(y) * (1.0 / hd)
            var = half_sum(yc * yc) * (1.0 / hd)
            out = _silu(rows(g_r, r0).astype(F32)) * (yc * lax.rsqrt(var + EPS))
            o_r[0, pl.ds(r0, c), :] = out.astype(o_r.dtype)
        return state * cd_f + kv_outer(k, v, ze_f)

    zero = jnp.zeros((PAIR, PAIR), F32)
    st = zero
    for cc in range(n_ctx - 1, -1, -1):
        st = bwd_chunk(kc_ref, vc_ref, cc * c, cc, st)

    def bwd_body(i, s):
        cl = n_lat - 1 - i
        return bwd_chunk(k_ref, v_ref, pl.multiple_of(cl * c, c), n_ctx + cl, s)

    lax.fori_loop(0, n_lat, bwd_body, st, unroll=4)

    st = zero
    for cc in range(n_ctx):
        st = fwd_chunk(qc_ref, kc_ref, vc_ref, gc_ref, yc_ref, cc * c, cc, st)

    def fwd_body(cl, s):
        return fwd_chunk(q_ref, k_ref, v_ref, g_ref, y_ref, pl.multiple_of(cl * c, c), n_ctx + cl, s)

    lax.fori_loop(0, n_lat, fwd_body, st, unroll=4)


def _retention(dec, p, pc, need_ctx):
    bsz, seq, w = p["ret_q"].shape
    t = pc["ret_k"].shape[1]
    n_pairs = w // PAIR
    lat = pl.BlockSpec((1, seq, PAIR), lambda b, h: (b, 0, h))
    ctx = pl.BlockSpec((1, t, PAIR), lambda b, h: (b, 0, h))
    out_specs, out_shape = [lat], [jax.ShapeDtypeStruct((bsz, seq, w), BF16)]
    if need_ctx:
        out_specs.append(ctx)
        out_shape.append(jax.ShapeDtypeStruct((bsz, t, w), BF16))
        qc, gc = pc["ret_q"], pc["ret_g"]
    else:
        qc, gc = pc["ret_k"], pc["ret_k"]
    n_chunks = (seq + t) // RET_CHUNK
    outs = pl.pallas_call(
        functools.partial(_ret_kernel, need_ctx=need_ctx),
        grid=(bsz, n_pairs),
        in_specs=[pl.BlockSpec(memory_space=pltpu.SMEM), lat, lat, lat, lat, ctx, ctx, ctx, ctx],
        out_specs=out_specs,
        out_shape=out_shape,
        scratch_shapes=[pltpu.VMEM((n_chunks, PAIR, PAIR), F32)],
        compiler_params=_params(("parallel", "parallel"), 32),
        name="retention",
    )(dec, p["ret_q"], p["ret_k"], p["ret_v"], p["ret_g"], qc, pc["ret_k"], pc["ret_v"], gc)
    return (outs[0], outs[1]) if need_ctx else (outs[0], None)


def _conv_kernel(w_ref, b_ref, c_ref, h_ref, o_ref):
    seq = b_ref.shape[1]
    rows = min(256, seq)
    halo = 16
    w0, w1, w2 = w_ref[0:1, :], w_ref[1:2, :], w_ref[2:3, :]
    for r0 in range(0, seq, rows):
        lo, hi = max(0, r0 - halo), min(seq, r0 + rows + halo)
        u = c_ref[0, lo:hi, :].astype(F32) * h_ref[0, lo:hi, :].astype(F32)
        n = hi - lo
        t = lo + lax.broadcasted_iota(jnp.int32, (n, 1), 0)
        prev = jnp.where(t == 0, 0.0, pltpu.roll(u, 1, 0))
        nxt = jnp.where(t == seq - 1, 0.0, pltpu.roll(u, n - 1, 0))
        y = prev * w0 + u * w1 + nxt * w2
        y = y[r0 - lo:r0 - lo + rows]
        o_ref[0, r0:r0 + rows, :] = (b_ref[0, r0:r0 + rows, :].astype(F32) * y).astype(o_ref.dtype)


def _short_conv(conv_w, b, c, h):
    bsz, seq, w = b.shape
    blk = pl.BlockSpec((1, seq, w), lambda i: (i, 0, 0))
    return pl.pallas_call(
        _conv_kernel,
        grid=(bsz,),
        in_specs=[pl.BlockSpec((3, w), lambda i: (0, 0)), blk, blk, blk],
        out_specs=blk,
        out_shape=jax.ShapeDtypeStruct((bsz, seq, w), BF16),
        compiler_params=_params(("parallel",), 48),
        name="short_conv",
    )(conv_w, b, c, h)


def _softmax_pv(scores, values, extra=None):
    m = functools.reduce(jnp.maximum, [jnp.max(s, axis=-1, keepdims=True) for s in scores])
    if extra is not None:
        m = jnp.maximum(m, extra)
    den = jnp.exp(extra - m) if extra is not None else 0.0
    acc = 0.0
    for s, v in zip(scores, values):
        p = jnp.exp(s - m)
        den = den + jnp.sum(p, axis=-1, keepdims=True)
        acc = acc + _dot(p.astype(BF16), v)
    return acc / den


def _swa_group(q_ref, j, k, v, lo, extra_scores, extra_values, sink_ref, valid):
    m = q_ref.shape[1]
    grp = SWA_HEADS // SWA_KV_HEADS
    qs = []
    for pp in range(grp // 2):
        qp = q_ref[0, :, (j * grp // 2 + pp) * PAIR:(j * grp // 2 + pp + 1) * PAIR]
        qs += [_keep_head(qp, lo, 0), _keep_head(qp, lo, 1)]
    qg = jnp.concatenate(qs, axis=0)
    sink = jnp.concatenate([jnp.full((m, 1), sink_ref[j * grp + g], F32) for g in range(grp)], axis=0)
    scores, values = [], []
    if k is not None:
        scores.append(jnp.where(valid, _dot_nt(qg, k), NEG_INF))
        values.append(v)
    for ke, ve in zip(extra_scores, extra_values):
        scores.append(_dot_nt(qg, ke))
        values.append(ve)
    o = _softmax_pv(scores, values, sink)
    return [jnp.where(lo, o[2 * pp * m:(2 * pp + 1) * m], o[(2 * pp + 1) * m:(2 * pp + 2) * m])
            for pp in range(grp // 2)]


def _swa_kernel(sink_ref, q_ref, k_ref, v_ref, kc_ref, vc_ref, o_ref):
    blk = SWA_BLOCK
    win = 3 * blk
    seq = k_ref.shape[1]
    grp = SWA_HEADS // SWA_KV_HEADS
    i = pl.program_id(1)
    start = pl.multiple_of(jnp.clip((i - 1) * blk, 0, seq - win), blk)
    lo = _first_head_lanes()
    qpos = i * blk + lax.broadcasted_iota(jnp.int32, (blk, win), 0)
    kpos = start + lax.broadcasted_iota(jnp.int32, (blk, win), 1)
    valid = jnp.abs(kpos - qpos) <= SWA_WINDOW
    valid = jnp.concatenate([valid] * grp, axis=0)
    for j in range(SWA_KV_HEADS):
        sl = slice(j * PAIR, (j + 1) * PAIR)
        outs = _swa_group(q_ref, j, k_ref[0, pl.ds(start, win), sl], v_ref[0, pl.ds(start, win), sl], lo,
                          [kc_ref[0, :, sl]], [vc_ref[0, :, sl]], sink_ref, valid)
        for pp, o in enumerate(outs):
            o_ref[0, :, (j * grp // 2 + pp) * PAIR:(j * grp // 2 + pp + 1) * PAIR] = o.astype(o_ref.dtype)


def _swa(sink, p, pc):
    bsz, seq, w = p["swa_q"].shape
    t = pc["swa_k"].shape[1]
    kvw = SWA_KV_HEADS * PAIR
    kv = pl.BlockSpec((1, seq, kvw), lambda b, i: (b, 0, 0))
    kvc = pl.BlockSpec((1, t, kvw), lambda b, i: (b, 0, 0))
    qb = pl.BlockSpec((1, SWA_BLOCK, w), lambda b, i: (b, i, 0))
    return pl.pallas_call(
        _swa_kernel,
        grid=(bsz, seq // SWA_BLOCK),
        in_specs=[pl.BlockSpec(memory_space=pltpu.SMEM), qb, kv, kv, kvc, kvc],
        out_specs=qb,
        out_shape=jax.ShapeDtypeStruct((bsz, seq, w), BF16),
        compiler_params=_params(("parallel", "parallel"), 32),
        name="swa",
    )(sink, p["swa_q"], p["swa_k"], p["swa_v"], pc["swa_k"], pc["swa_v"])


NA_GROUP = 4
NA_BAND = 12
NA_GRID_ROWS = 32
_NA_CFGS = ((0, 0), (4, 0), (8, 4), (28, 20))


def _na_bias_kernel(rpb_ref, o_ref):
    h = pl.program_id(0)
    gw, wc = GRID_W, NA_COLS
    rows = NA_GRID_ROWS
    cq = lax.broadcasted_iota(jnp.int32, (gw, gw), 0)
    ck = lax.broadcasted_iota(jnp.int32, (gw, gw), 1)
    col_start = jnp.clip(cq - wc // 2, 0, gw - wc)
    col_ok = (ck >= col_start) & (ck < col_start + wc)
    dc = jnp.clip(ck - cq, -(wc - 1), wc - 1) + (wc - 1)
    neg = jnp.full((gw, gw), NEG_INF, F32)
    per_dr = []
    for dr in range(2 * NA_ROWS - 1):
        t = jnp.zeros((gw, gw), F32)
        for d in range(2 * wc - 1):
            t = jnp.where(dc == d, rpb_ref[h, dr, d], t)
        per_dr.append(jnp.where(col_ok, t, NEG_INF))
    for c, (r0, start) in enumerate(_NA_CFGS):
        for i in range(NA_GROUP):
            r = r0 + i
            band0 = min(max(r - NA_ROWS // 2, 0), rows - NA_ROWS)
            blocks = []
            for jp in range(NA_BAND):
                j = start + jp
                blocks.append(per_dr[j - r + NA_ROWS - 1] if band0 <= j < band0 + NA_ROWS else neg)
            o_ref[c, 0, i * gw:(i + 1) * gw, :] = jnp.concatenate(blocks, axis=1)


def _na_bias(rpb):
    n_cfg = len(_NA_CFGS)
    return pl.pallas_call(
        _na_bias_kernel,
        grid=(NA_HEADS,),
        in_specs=[pl.BlockSpec(memory_space=pltpu.SMEM)],
        out_specs=pl.BlockSpec((n_cfg, 1, NA_GROUP * GRID_W, NA_BAND * GRID_W), lambda h: (0, h, 0, 0)),
        out_shape=jax.ShapeDtypeStruct((n_cfg, NA_HEADS, NA_GROUP * GRID_W, NA_BAND * GRID_W), F32),
        compiler_params=_params(("parallel",), 32),
        name="na_bias",
    )(rpb)


def _na_kernel(q_ref, k_ref, v_ref, kc_ref, vc_ref, bias_ref, o_ref):
    rows = k_ref.shape[1] // GRID_W
    nk = NA_BAND * GRID_W
    g = pl.program_id(1)
    first = jnp.clip(g * NA_GROUP - NA_ROWS // 2, 0, rows - NA_BAND)
    st = pl.multiple_of(first * GRID_W, GRID_W)
    lo = _first_head_lanes()
    for hp in range(NA_HEADS // 2):
        sl = slice(hp * PAIR, (hp + 1) * PAIR)
        qp = q_ref[0, :, sl]
        kb, vb = k_ref[0, pl.ds(st, nk), sl], v_ref[0, pl.ds(st, nk), sl]
        kc, vc = kc_ref[0, :, sl], vc_ref[0, :, sl]
        halves = []
        for j in range(2):
            qm = _keep_head(qp, lo, j)
            s_nb = _dot_nt(qm, kb) + bias_ref[0, 2 * hp + j]
            halves.append(_softmax_pv([s_nb, _dot_nt(qm, kc)], [vb, vc]))
        o_ref[0, :, sl] = jnp.where(lo, halves[0], halves[1]).astype(o_ref.dtype)


def _na(bias, p, pc):
    bsz, seq, w = p["na_q"].shape
    t = pc["na_k"].shape[1]
    rows = seq // GRID_W
    assert rows == NA_GRID_ROWS and bias.shape[0] == len(_NA_CFGS)
    n_groups = rows // NA_GROUP
    kv = pl.BlockSpec((1, seq, w), lambda b, g: (b, 0, 0))
    kvc = pl.BlockSpec((1, t, w), lambda b, g: (b, 0, 0))
    qb = pl.BlockSpec((1, NA_GROUP * GRID_W, w), lambda b, g: (b, g, 0))

    def cfg(b, g):
        return (jnp.where(g < 2, g, jnp.where(g < n_groups - 1, 2, 3)), 0, 0, 0)

    return pl.pallas_call(
        _na_kernel,
        grid=(bsz, n_groups),
        in_specs=[qb, kv, kv, kvc, kvc,
                  pl.BlockSpec((1, NA_HEADS, NA_GROUP * GRID_W, NA_BAND * GRID_W), cfg)],
        out_specs=qb,
        out_shape=jax.ShapeDtypeStruct((bsz, seq, w), BF16),
        compiler_params=_params(("parallel", "arbitrary"), 48),
        name="na",
    )(p["na_q"], p["na_k"], p["na_v"], pc["na_k"], pc["na_v"], bias)


def _ctx_attn_kernel(sink_ref, sq_ref, sk_ref, sv_ref, nq_ref, nk_ref, nv_ref, so_ref, no_ref):
    grp = SWA_HEADS // SWA_KV_HEADS
    lo = _first_head_lanes()
    for j in range(SWA_KV_HEADS):
        sl = slice(j * PAIR, (j + 1) * PAIR)
        outs = _swa_group(sq_ref, j, None, None, lo, [sk_ref[0, :, sl]], [sv_ref[0, :, sl]], sink_ref, None)
        for pp, o in enumerate(outs):
            so_ref[0, :, (j * grp // 2 + pp) * PAIR:(j * grp // 2 + pp + 1) * PAIR] = o.astype(so_ref.dtype)
    for hp in range(NA_HEADS // 2):
        sl = slice(hp * PAIR, (hp + 1) * PAIR)
        qp, k, v = nq_ref[0, :, sl], nk_ref[0, :, sl], nv_ref[0, :, sl]
        halves = [_softmax_pv([_dot_nt(_keep_head(qp, lo, j), k)], [v]) for j in range(2)]
        no_ref[0, :, sl] = jnp.where(lo, halves[0], halves[1]).astype(no_ref.dtype)


def _ctx_attn(sink, pc):
    bsz, t, w = pc["swa_q"].shape
    kvw = SWA_KV_HEADS * PAIR
    full = pl.BlockSpec((1, t, w), lambda b: (b, 0, 0))
    kv = pl.BlockSpec((1, t, kvw), lambda b: (b, 0, 0))
    return pl.pallas_call(
        _ctx_attn_kernel,
        grid=(bsz,),
        in_specs=[pl.BlockSpec(memory_space=pltpu.SMEM), full, kv, kv, full, full, full],
        out_specs=[full, full],
        out_shape=[jax.ShapeDtypeStruct((bsz, t, w), BF16)] * 2,
        compiler_params=_params(("parallel",), 32),
        name="ctx_attn",
    )(sink, pc["swa_q"], pc["swa_k"], pc["swa_v"], pc["na_q"], pc["na_k"], pc["na_v"])


def _outproj_kernel(yr_ref, yc_ref, ys_ref, yn_ref, w_ref, x_ref, g1_ref, gain_ref, sc_ref, sh_ref, wr_ref,
                    xo_ref, h_ref, lg_ref):
    gw = GROUP_WIDTH
    acc = _dot(yr_ref[0], w_ref[0:gw, :])
    acc += _dot(yc_ref[0], w_ref[gw:2 * gw, :])
    acc += _dot(ys_ref[0], w_ref[2 * gw:3 * gw, :])
    acc += _dot(yn_ref[0], w_ref[3 * gw:4 * gw, :])
    xn = x_ref[0] + g1_ref[0] * acc
    xo_ref[0] = xn
    hb = _rms_modulate(xn, gain_ref[...], sc_ref[0], sh_ref[0]).astype(BF16)
    h_ref[0] = hb
    lg_ref[0] = _dot(hb, wr_ref[...])


def _outproj(ys, w_out, x, g1, gain, sc, sh, w_router, *, tm):
    bsz, seq, d = x.shape
    yb = pl.BlockSpec((1, tm, GROUP_WIDTH), lambda b, i: (b, i, 0))
    xb = pl.BlockSpec((1, tm, d), lambda b, i: (b, i, 0))
    vec = pl.BlockSpec((1, 1, d), lambda b, i: (b, 0, 0))
    return pl.pallas_call(
        _outproj_kernel,
        grid=(bsz, seq // tm),
        in_specs=[yb, yb, yb, yb,
                  pl.BlockSpec((d, d), lambda b, i: (0, 0), pipeline_mode=pl.Buffered(1)),
                  xb, vec, pl.BlockSpec((1, d), lambda b, i: (0, 0)), vec, vec,
                  pl.BlockSpec((d, N_EXPERTS), lambda b, i: (0, 0))],
        out_specs=[xb, xb, pl.BlockSpec((1, tm, N_EXPERTS), lambda b, i: (b, i, 0))],
        out_shape=[jax.ShapeDtypeStruct((bsz, seq, d), F32), jax.ShapeDtypeStruct((bsz, seq, d), BF16),
                   jax.ShapeDtypeStruct((bsz, seq, N_EXPERTS), F32)],
        compiler_params=_params(("parallel", "parallel"), 48),
        name="outproj",
    )(*ys, w_out, x, g1, gain.reshape(1, d), sc, sh, w_router.astype(BF16))


def _cumsum_lanes(x, tri):
    outs = []
    carry = jnp.zeros((x.shape[0], 1), F32)
    for c0 in range(0, x.shape[1], 128):
        cs = _dot(x[:, c0:c0 + 128], tri) + carry
        outs.append(cs)
        carry = cs[:, 127:128]
    return jnp.concatenate(outs, axis=1) if len(outs) > 1 else outs[0]


def _router_kernel(lg_ref, pos_ref, gate_ref, *, cap):
    lg = lg_ref[0]
    e = jnp.exp(lg - jnp.max(lg, axis=0, keepdims=True))
    aff = e / jnp.sum(e, axis=0, keepdims=True)
    capf = float(cap)
    lo = jnp.zeros((lg.shape[0], 1), jnp.int32)
    for bit in range(30, -1, -1):
        cand = lo | (1 << bit)
        cnt = jnp.sum(jnp.where(aff >= pltpu.bitcast(cand, F32), 1.0, 0.0), axis=1, keepdims=True)
        lo = jnp.where(cnt >= capf, cand, lo)
    kth = pltpu.bitcast(lo, F32)
    gt = aff > kth
    eq = aff == kth
    need = capf - jnp.sum(jnp.where(gt, 1.0, 0.0), axis=1, keepdims=True)
    ri = lax.broadcasted_iota(jnp.int32, (128, 128), 0)
    ci = lax.broadcasted_iota(jnp.int32, (128, 128), 1)
    tri = jnp.where(ri <= ci, 1.0, 0.0).astype(BF16)
    eq_rank = _cumsum_lanes(jnp.where(eq, 1.0, 0.0).astype(BF16), tri)
    sel = gt | (eq & (eq_rank <= need))
    slot = _cumsum_lanes(jnp.where(sel, 1.0, 0.0).astype(BF16), tri) - 1.0
    pos_ref[0] = jnp.where(sel, slot, -1.0).astype(jnp.int32)
    gate_ref[0] = aff


def _router(logits_t, cap):
    bsz, n_e, t = logits_t.shape
    blk = pl.BlockSpec((1, n_e, t), lambda b: (b, 0, 0))
    return pl.pallas_call(
        functools.partial(_router_kernel, cap=cap),
        grid=(bsz,),
        in_specs=[blk],
        out_specs=[blk, blk],
        out_shape=[jax.ShapeDtypeStruct((bsz, n_e, t), jnp.int32), jax.ShapeDtypeStruct((bsz, n_e, t), F32)],
        compiler_params=_params(("parallel",), 32),
        name="router",
    )(logits_t)


def _expert_kernel(pos_ref, gate_ref, h_ref, wg_ref, wu_ref, wd_ref, o_ref, *, cap):
    t = h_ref.shape[1]
    hit = lax.broadcasted_iota(jnp.int32, (cap, t), 0) == pos_ref[0, 0]
    xs = _dot(jnp.where(hit, 1.0, 0.0).astype(BF16), h_ref[0]).astype(BF16)
    a = _dot(xs, wg_ref[0])
    u = _dot(xs, wu_ref[0])
    ye = _dot((_silu(a) * u).astype(BF16), wd_ref[0])
    gate = jnp.sum(jnp.where(hit, gate_ref[0, 0], 0.0), axis=1, keepdims=True)
    o_ref[0, 0] = (ye * gate).astype(o_ref.dtype)


def _experts(pos, gate, h, wg, wu, wd, cap):
    bsz, t, d = h.shape
    n_e, _, ff = wg.shape
    row = pl.BlockSpec((1, 1, 1, t), lambda e, b: (b, e, 0, 0))
    return pl.pallas_call(
        functools.partial(_expert_kernel, cap=cap),
        grid=(n_e, bsz),
        in_specs=[row, row, pl.BlockSpec((1, t, d), lambda e, b: (b, 0, 0)),
                  pl.BlockSpec((1, d, ff), lambda e, b: (e, 0, 0)),
                  pl.BlockSpec((1, d, ff), lambda e, b: (e, 0, 0)),
                  pl.BlockSpec((1, ff, d), lambda e, b: (e, 0, 0))],
        out_specs=pl.BlockSpec((1, 1, cap, d), lambda e, b: (b, e, 0, 0)),
        out_shape=jax.ShapeDtypeStruct((bsz, n_e, cap, d), BF16),
        compiler_params=_params(("arbitrary", "arbitrary"), 56),
        name="experts",
    )(pos.reshape(bsz, n_e, 1, t), gate.reshape(bsz, n_e, 1, t), h, wg, wu, wd)


def _combine_kernel(pos_ref, ye_ref, x_ref, g2_ref, *rest, cap):
    o_ref = rest[-1]
    n_e = pos_ref.shape[1]
    tb = x_ref.shape[1]
    slot = lax.broadcasted_iota(jnp.int32, (cap, tb), 0)
    hit = jnp.concatenate([jnp.where(slot == pos_ref[0, e:e + 1, :], 1.0, 0.0).astype(BF16) for e in range(n_e)],
                          axis=0)
    xn = x_ref[0] + g2_ref[0] * _dot_tn(hit, ye_ref[0])
    if len(rest) == 2:
        ms = jnp.mean(xn * xn, axis=-1, keepdims=True)
        xn = (xn * lax.rsqrt(ms + EPS)) * rest[0][...]
    o_ref[0] = xn


def _combine(pos, ye, x, g2, cap, tb, final_gain=None):
    bsz, t, d = x.shape
    n_e = pos.shape[1]
    xb = pl.BlockSpec((1, tb, d), lambda b, i: (b, i, 0))
    in_specs = [pl.BlockSpec((1, n_e, tb), lambda b, i: (b, 0, i)),
                pl.BlockSpec((1, n_e * cap, d), lambda b, i: (b, 0, 0)),
                xb, pl.BlockSpec((1, 1, d), lambda b, i: (b, 0, 0))]
    args = [pos, ye.reshape(bsz, n_e * cap, d), x, g2]
    if final_gain is not None:
        in_specs.append(pl.BlockSpec((1, d), lambda b, i: (0, 0)))
        args.append(final_gain.reshape(1, d))
    return pl.pallas_call(
        functools.partial(_combine_kernel, cap=cap),
        grid=(bsz, t // tb),
        in_specs=in_specs,
        out_specs=xb,
        out_shape=jax.ShapeDtypeStruct((bsz, t, d), F32),
        compiler_params=_params(("parallel", "arbitrary"), 56),
        name="combine",
    )(*args)


def _moe(x, h, logits, g2, wg, wu, wd, final_gain=None):
    bsz, t, d = x.shape
    cap = EC_CAPACITY_FACTOR * t // N_EXPERTS
    pos, gate = _router(jnp.swapaxes(logits, 1, 2), cap)
    ye = _experts(pos, gate, h, wg, wu, wd, cap)
    return _combine(pos, ye, x, g2, cap, min(t, 512), final_gain)


def _layer(x, xc, mods, mods_c, lw, need_ctx, final_gain):
    sh1, sc1, g1, sh2, sc2, g2 = mods
    sh1c, sc1c, g1c, sh2c, sc2c, g2c = mods_c
    names_lat = tuple(p[0] for p in _PIECES)
    p = _inproj(x, lw["norm_mix"], sc1, sh1, lw["w_in"], rope=True, names=names_lat, tm=256)
    pc = _inproj(xc, lw["norm_mix"], sc1c, sh1c, lw["w_in"], rope=False,
                 names=names_lat if need_ctx else _CTX_KV_ONLY, tm=256)

    y_ret, yc_ret = _retention(lw["ret_decay"], p, pc, need_ctx)
    y_conv = _short_conv(lw["conv_w"], p["conv_b"], p["conv_c"], p["conv_h"])
    y_swa = _swa(lw["swa_sink"], p, pc)
    y_na = _na(_na_bias(lw["na_rpb"]), p, pc)
    x, h, logits = _outproj((y_ret, y_conv, y_swa, y_na), lw["w_out"], x, g1, lw["norm_ffn"], sc2, sh2,
                            lw["w_router"], tm=256)
    x = _moe(x, h, logits, g2, lw["w_gate"], lw["w_up"], lw["w_down"], final_gain)
    if need_ctx:
        yc_conv = _short_conv(lw["conv_w"], pc["conv_b"], pc["conv_c"], pc["conv_h"])
        yc_swa, yc_na = _ctx_attn(lw["swa_sink"], pc)
        xc, hc, logits_c = _outproj((yc_ret, yc_conv, yc_swa, yc_na), lw["w_out"], xc, g1c, lw["norm_ffn"],
                                    sc2c, sh2c, lw["w_router"], tm=256)
        xc = _moe(xc, hc, logits_c, g2c, lw["w_gate"], lw["w_up"], lw["w_down"])
    return x, xc


def kernel(x, c, ctx, c_ctx, w_ada, b_ada, norm_mix, norm_ffn, w_in, w_out, ret_decay_fwd, ret_decay_bwd,
           conv_w, swa_sink, na_rpb, w_router, w_gate, w_up, w_down, norm_final):
    bsz, _, d = x.shape
    depth = w_ada.shape[0]
    assert bsz + 1 <= ADA_ROWS
    cs = jnp.zeros((ADA_ROWS, d), F32).at[:bsz].set(c).at[bsz].set(c_ctx)
    ada = _ada(cs, w_ada, b_ada)
    w_out_b = w_out.astype(BF16)
    wg_b, wu_b, wd_b = w_gate.astype(BF16), w_up.astype(BF16), w_down.astype(BF16)
    xc = ctx
    for l in range(depth):
        mods = tuple(ada[l, :bsz, i * d:(i + 1) * d].reshape(bsz, 1, d) for i in range(6))
        mods_c = tuple(jnp.broadcast_to(ada[l, bsz, i * d:(i + 1) * d].reshape(1, 1, d), (bsz, 1, d))
                       for i in range(6))
        lw = dict(norm_mix=norm_mix[l], norm_ffn=norm_ffn[l], w_in=_widen_swa_kv(w_in[l].astype(BF16)),
                  w_out=w_out_b[l], ret_decay=jnp.stack([ret_decay_fwd[l], ret_decay_bwd[l]]), conv_w=conv_w[l],
                  swa_sink=swa_sink[l], na_rpb=na_rpb[l], w_router=w_router[l],
                  w_gate=wg_b[l], w_up=wu_b[l], w_down=wd_b[l])
        last = l == depth - 1
        x, xc = _layer(x, xc, mods, mods_c, lw, need_ctx=not last, final_gain=norm_final if last else None)
    return x
```

```python
import functools

import numpy as np
import jax
import jax.numpy as jnp
from jax import lax
from jax.experimental import pallas as pl
from jax.experimental.pallas import tpu as pltpu

D_MODEL = 2048
DEPTH = 2
GRID_W = 64
HEAD_DIM = 64
GROUP_WIDTH = D_MODEL // 4
RET_CHUNK = 128
SWA_HEADS = 8
SWA_KV_HEADS = 2
SWA_WINDOW = 128
SWA_BLOCK = 128
NA_HEADS = 8
NA_ROWS = 8
NA_COLS = 16
N_EXPERTS = 16
EXPERT_FF = D_MODEL // 2
EC_CAPACITY_FACTOR = 2
ROPE_BASE = 10000.0
EPS = 1e-6
NEG_INF = -1e30
F32 = jnp.float32
BF16 = jnp.bfloat16
ADA_ROWS = 16
QK_SCALE = HEAD_DIM ** -0.5
PAIR = 2 * HEAD_DIM

_PIECES = (
    ("ret_q", 0, 512, True, 1.0),
    ("ret_k", 512, 512, True, QK_SCALE),
    ("ret_v", 1024, 512, False, 1.0),
    ("ret_g", 1536, 512, False, 1.0),
    ("conv_b", 2048, 512, False, 1.0),
    ("conv_c", 2560, 512, False, 1.0),
    ("conv_h", 3072, 512, False, 1.0),
    ("swa_q", 3584, 512, True, QK_SCALE),
    ("swa_k", 4096, 256, True, 1.0),
    ("swa_v", 4352, 256, False, 1.0),
    ("na_q", 4608, 512, False, QK_SCALE),
    ("na_k", 5120, 512, False, 1.0),
    ("na_v", 5632, 512, False, 1.0),
)
IN_COLS = 6144
_SWA_KV_COL = 4096
_CTX_KV_ONLY = ("ret_k", "ret_v", "swa_k", "swa_v", "na_k", "na_v")


def _widen_swa_kv(w):
    hd, c0 = HEAD_DIM, _SWA_KV_COL
    parts = [w[:, :c0]]
    for base in (c0, c0 + SWA_KV_HEADS * hd):
        for j in range(SWA_KV_HEADS):
            head = w[:, base + j * hd:base + (j + 1) * hd]
            parts += [head, head]
    parts.append(w[:, c0 + 2 * SWA_KV_HEADS * hd:])
    return jnp.concatenate(parts, axis=1)


def _params(sem, vmem_mb):
    return pltpu.CompilerParams(dimension_semantics=sem, vmem_limit_bytes=vmem_mb << 20)


def _silu(x):
    return x * jax.nn.sigmoid(x)


def _dot(a, b):
    return jnp.dot(a, b, preferred_element_type=F32)


def _dot_nt(a, b):
    return lax.dot_general(a, b, (((1,), (1,)), ((), ())), preferred_element_type=F32)


def _dot_tn(a, b):
    return lax.dot_general(a, b, (((0,), (0,)), ((), ())), preferred_element_type=F32)


def _first_head_lanes():
    return lax.broadcasted_iota(jnp.int32, (1, PAIR), 1) < HEAD_DIM


def _keep_head(x, lo, j):
    return jnp.where(lo if j == 0 else jnp.logical_not(lo), x, jnp.zeros_like(x))


def _ada_kernel(c_ref, w_ref, b_ref, o_ref):
    s = _silu(c_ref[...]).astype(BF16)
    o_ref[0] = _dot(s, w_ref[0].astype(BF16)) + b_ref[0]


def _ada(cs, w_ada, b_ada):
    depth, d, n = w_ada.shape
    tn = 1024
    return pl.pallas_call(
        _ada_kernel,
        grid=(depth, n // tn),
        in_specs=[
            pl.BlockSpec((ADA_ROWS, d), lambda l, j: (0, 0)),
            pl.BlockSpec((1, d, tn), lambda l, j: (l, 0, j)),
            pl.BlockSpec((1, 1, tn), lambda l, j: (l, 0, j)),
        ],
        out_specs=pl.BlockSpec((1, ADA_ROWS, tn), lambda l, j: (l, 0, j)),
        out_shape=jax.ShapeDtypeStruct((depth, ADA_ROWS, n), F32),
        compiler_params=_params(("arbitrary", "arbitrary"), 40),
        name="ada",
    )(cs, w_ada, b_ada.reshape(depth, 1, n))


def _rope_tables(seq):
    half = HEAD_DIM // 2
    nf = half // 2
    t = np.arange(seq)
    row = (t // GRID_W).astype(np.float32)
    col = (t % GRID_W).astype(np.float32)
    inv = (np.float32(ROPE_BASE) ** (-np.arange(nf, dtype=np.float32) / np.float32(nf))).astype(np.float32)
    ang = np.concatenate([row[:, None] * inv, col[:, None] * inv], axis=-1).astype(np.float32)
    cos, sin = np.cos(ang), np.sin(ang)
    cos64 = np.concatenate([cos, cos], axis=-1)
    sin64 = np.concatenate([-sin, sin], axis=-1)
    return (np.tile(cos64, (1, 2)).astype(np.float32), np.tile(sin64, (1, 2)).astype(np.float32))


def _rope(z, cos, sin_signed):
    w = z.shape[1]
    reps = w // 128
    cos_w = jnp.concatenate([cos] * reps, axis=1) if reps > 1 else cos
    sin_w = jnp.concatenate([sin_signed] * reps, axis=1) if reps > 1 else sin_signed
    lane = lax.broadcasted_iota(jnp.int32, (1, w), 1)
    first_half = (lane & (HEAD_DIM - 1)) < (HEAD_DIM // 2)
    up = pltpu.roll(z, w - HEAD_DIM // 2, 1)
    dn = pltpu.roll(z, HEAD_DIM // 2, 1)
    return z * cos_w + jnp.where(first_half, up, dn) * sin_w


def _rms_modulate(x, gain, scale, shift):
    ms = jnp.mean(x * x, axis=-1, keepdims=True)
    return (x * lax.rsqrt(ms + EPS)) * (gain * (1.0 + scale)) + shift


def _inproj_kernel(*refs, pieces, rope):
    if rope:
        x_ref, gain_ref, sc_ref, sh_ref, cos_ref, sin_ref, w_ref = refs[:7]
        o_refs = refs[7:]
    else:
        x_ref, gain_ref, sc_ref, sh_ref, w_ref = refs[:5]
        o_refs = refs[5:]
    hb = _rms_modulate(x_ref[0], gain_ref[...], sc_ref[0], sh_ref[0]).astype(BF16)
    for (_, off, width, do_rope, scale), o_ref in zip(pieces, o_refs):
        z = _dot(hb, w_ref[:, off:off + width])
        if rope and do_rope:
            z = _rope(z, cos_ref[...], sin_ref[...])
        if scale != 1.0:
            z = z * scale
        o_ref[0] = z.astype(BF16)


def _inproj(x, gain, sc, sh, w_in, *, rope, names, tm):
    bsz, seq, d = x.shape
    pieces = tuple(p for p in _PIECES if p[0] in names)
    vec = pl.BlockSpec((1, 1, d), lambda b, i: (b, 0, 0))
    in_specs = [pl.BlockSpec((1, tm, d), lambda b, i: (b, i, 0)),
                pl.BlockSpec((1, d), lambda b, i: (0, 0)), vec, vec]
    args = [x, gain.reshape(1, d), sc, sh]
    if rope:
        cos, sin = _rope_tables(seq)
        in_specs += [pl.BlockSpec((tm, 128), lambda b, i: (i, 0))] * 2
        args += [jnp.asarray(cos), jnp.asarray(sin)]
    in_specs.append(pl.BlockSpec((d, IN_COLS), lambda b, i: (0, 0), pipeline_mode=pl.Buffered(1)))
    args.append(w_in)
    outs = pl.pallas_call(
        functools.partial(_inproj_kernel, pieces=pieces, rope=rope),
        grid=(bsz, seq // tm),
        in_specs=in_specs,
        out_specs=[pl.BlockSpec((1, tm, p[2]), lambda b, i: (b, i, 0)) for p in pieces],
        out_shape=[jax.ShapeDtypeStruct((bsz, seq, p[2]), BF16) for p in pieces],
        compiler_params=_params(("parallel", "parallel"), 56),
        name="inproj_lat" if rope else "inproj_ctx",
    )(*args)
    return {p[0]: o for p, o in zip(pieces, outs)}


def _log_sigmoid(x):
    return jnp.minimum(x, 0.0) - jnp.log(1.0 + jnp.exp(-jnp.abs(x)))


def _ret_kernel(dec_ref, q_ref, k_ref, v_ref, g_ref, qc_ref, kc_ref, vc_ref, gc_ref, *rest, need_ctx):
    if need_ctx:
        y_ref, yc_ref, sb_ref = rest
    else:
        (y_ref, sb_ref), yc_ref = rest, None
    c = RET_CHUNK
    hd = HEAD_DIM
    n_lat = q_ref.shape[1] // c
    n_ctx = qc_ref.shape[1] // c
    hp = pl.program_id(1)

    lo = _first_head_lanes()
    ri = lax.broadcasted_iota(jnp.int32, (c, c), 0)
    ci = lax.broadcasted_iota(jnp.int32, (c, c), 1)
    diff = (ri - ci).astype(F32)
    same_head = (ri < hd) == (ci < hd)
    pos = lax.broadcasted_iota(jnp.int32, (c, PAIR), 0).astype(F32)

    lg_f = [_log_sigmoid(jnp.full((1, c), dec_ref[0, 2 * hp + j], F32)) for j in range(2)]
    lg_b = [_log_sigmoid(jnp.full((1, c), dec_ref[1, 2 * hp + j], F32)) for j in range(2)]
    dmat = [jnp.where(diff >= 0.0, jnp.exp(lg_f[j] * jnp.maximum(diff, 0.0)),
                      jnp.exp(lg_b[j] * jnp.maximum(-diff, 0.0))) for j in range(2)]
    lf = jnp.where(lo, lg_f[0], lg_f[1])
    lb = jnp.where(lo, lg_b[0], lg_b[1])
    xi_f, xi_b = jnp.exp(lf * (pos + 1.0)), jnp.exp(lb * (c - pos))
    ze_f, ze_b = jnp.exp(lf * (c - 1.0 - pos)), jnp.exp(lb * pos)
    cd_f, cd_b = jnp.exp(lf * float(c)), jnp.exp(lb * float(c))

    def rows(ref, r0):
        return ref[0, pl.ds(r0, c), :]

    def kv_outer(k, v, zeta):
        kz = (k.astype(F32) * zeta).astype(BF16)
        return jnp.where(same_head, _dot_tn(kz, v), 0.0)

    def half_sum(x):
        s_lo = jnp.sum(jnp.where(lo, x, 0.0), axis=-1, keepdims=True)
        s_all = jnp.sum(x, axis=-1, keepdims=True)
        return jnp.where(lo, s_lo, s_all - s_lo)

    def bwd_chunk(k_r, v_r, r0, slot, state):
        sb_ref[slot] = state
        return state * cd_b + kv_outer(rows(k_r, r0), rows(v_r, r0), ze_b)

    def fwd_chunk(q_r, k_r, v_r, g_r, o_r, r0, slot, state):
        k, v = rows(k_r, r0), rows(v_r, r0)
        if o_r is not None:
            q = rows(q_r, r0)
            intra = []
            for j in range(2):
                a = (_dot_nt(_keep_head(q, lo, j), k) * dmat[j]).astype(BF16)
                intra.append(_dot(a, v))
            y = (jnp.where(lo, intra[0], intra[1]) + _dot(q, state.astype(BF16)) * xi_f
                 + _dot(q, sb_ref[slot].astype(BF16)) * xi_b)
            yc = y - half_sum(y) * (1.0 / hd)
            var = half_sum(yc * yc) * (1.0 / hd)
            out = _silu(rows(g_r, r0).astype(F32)) * (yc * lax.rsqrt(var + EPS))
            o_r[0, pl.ds(r0, c), :] = out.astype(o_r.dtype)
        return state * cd_f + kv_outer(k, v, ze_f)

    zero = jnp.zeros((PAIR, PAIR), F32)
    st = zero
    for cc in range(n_ctx - 1, -1, -1):
        st = bwd_chunk(kc_ref, vc_ref, cc * c, cc, st)

    def bwd_body(i, s):
        cl = n_lat - 1 - i
        return bwd_chunk(k_ref, v_ref, pl.multiple_of(cl * c, c), n_ctx + cl, s)

    lax.fori_loop(0, n_lat, bwd_body, st, unroll=4)

    st = zero
    for cc in range(n_ctx):
        st = fwd_chunk(qc_ref, kc_ref, vc_ref, gc_ref, yc_ref, cc * c, cc, st)

    def fwd_body(cl, s):
        return fwd_chunk(q_ref, k_ref, v_ref, g_ref, y_ref, pl.multiple_of(cl * c, c), n_ctx + cl, s)

    lax.fori_loop(0, n_lat, fwd_body, st, unroll=4)


def _retention(dec, p, pc, need_ctx):
    bsz, seq, w = p["ret_q"].shape
    t = pc["ret_k"].shape[1]
    n_pairs = w // PAIR
    lat = pl.BlockSpec((1, seq, PAIR), lambda b, h: (b, 0, h))
    ctx = pl.BlockSpec((1, t, PAIR), lambda b, h: (b, 0, h))
    out_specs, out_shape = [lat], [jax.ShapeDtypeStruct((bsz, seq, w), BF16)]
    if need_ctx:
        out_specs.append(ctx)
        out_shape.append(jax.ShapeDtypeStruct((bsz, t, w), BF16))
        qc, gc = pc["ret_q"], pc["ret_g"]
    else:
        qc, gc = pc["ret_k"], pc["ret_k"]
    n_chunks = (seq + t) // RET_CHUNK
    outs = pl.pallas_call(
        functools.partial(_ret_kernel, need_ctx=need_ctx),
        grid=(bsz, n_pairs),
        in_specs=[pl.BlockSpec(memory_space=pltpu.SMEM), lat, lat, lat, lat, ctx, ctx, ctx, ctx],
        out_specs=out_specs,
        out_shape=out_shape,
        scratch_shapes=[pltpu.VMEM((n_chunks, PAIR, PAIR), F32)],
        compiler_params=_params(("parallel", "parallel"), 32),
        name="retention",
    )(dec, p["ret_q"], p["ret_k"], p["ret_v"], p["ret_g"], qc, pc["ret_k"], pc["ret_v"], gc)
    return (outs[0], outs[1]) if need_ctx else (outs[0], None)


def _conv_kernel(w_ref, b_ref, c_ref, h_ref, o_ref):
    seq = b_ref.shape[1]
    rows = min(256, seq)
    halo = 16
    w0, w1, w2 = w_ref[0:1, :], w_ref[1:2, :], w_ref[2:3, :]
    for r0 in range(0, seq, rows):
        lo, hi = max(0, r0 - halo), min(seq, r0 + rows + halo)
        u = c_ref[0, lo:hi, :].astype(F32) * h_ref[0, lo:hi, :].astype(F32)
        n = hi - lo
        t = lo + lax.broadcasted_iota(jnp.int32, (n, 1), 0)
        prev = jnp.where(t == 0, 0.0, pltpu.roll(u, 1, 0))
        nxt = jnp.where(t == seq - 1, 0.0, pltpu.roll(u, n - 1, 0))
        y = prev * w0 + u * w1 + nxt * w2
        y = y[r0 - lo:r0 - lo + rows]
        o_ref[0, r0:r0 + rows, :] = (b_ref[0, r0:r0 + rows, :].astype(F32) * y).astype(o_ref.dtype)


def _short_conv(conv_w, b, c, h):
    bsz, seq, w = b.shape
    blk = pl.BlockSpec((1, seq, w), lambda i: (i, 0, 0))
    return pl.pallas_call(
        _conv_kernel,
        grid=(bsz,),
        in_specs=[pl.BlockSpec((3, w), lambda i: (0, 0)), blk, blk, blk],
        out_specs=blk,
        out_shape=jax.ShapeDtypeStruct((bsz, seq, w), BF16),
        compiler_params=_params(("parallel",), 48),
        name="short_conv",
    )(conv_w, b, c, h)


def _softmax_pv_t(scores_t, values, extra=None):
    mx = functools.reduce(jnp.maximum, [jnp.max(s, axis=0, keepdims=True) for s in scores_t])
    if extra is not None:
        mx = jnp.maximum(mx, extra)
    den = jnp.exp(extra - mx) if extra is not None else 0.0
    acc = 0.0
    for s, v in zip(scores_t, values):
        p = jnp.exp(s - mx)
        den = den + jnp.sum(p, axis=0, keepdims=True)
        acc = acc + _dot_tn(v, p.astype(BF16))
    return (acc / den).T


def _swa_group(q_ref, j, k, v, lo, extra_keys, extra_values, sink_ref, valid_t):
    m = q_ref.shape[1]
    grp = SWA_HEADS // SWA_KV_HEADS
    qs = []
    for pp in range(grp // 2):
        qp = q_ref[0, :, (j * grp // 2 + pp) * PAIR:(j * grp // 2 + pp + 1) * PAIR]
        qs += [_keep_head(qp, lo, 0), _keep_head(qp, lo, 1)]
    qg = jnp.concatenate(qs, axis=0)
    sink = jnp.concatenate([jnp.full((1, m), sink_ref[j * grp + g], F32) for g in range(grp)], axis=1)
    scores, values = [], []
    if k is not None:
        scores.append(jnp.where(valid_t, _dot_nt(k, qg), NEG_INF))
        values.append(v)
    for ke, ve in zip(extra_keys, extra_values):
        scores.append(_dot_nt(ke, qg))
        values.append(ve)
    o = _softmax_pv_t(scores, values, sink)
    return [jnp.where(lo, o[2 * pp * m:(2 * pp + 1) * m], o[(2 * pp + 1) * m:(2 * pp + 2) * m])
            for pp in range(grp // 2)]


def _swa_kernel(sink_ref, q_ref, k_ref, v_ref, kc_ref, vc_ref, o_ref):
    blk = SWA_BLOCK
    win = 3 * blk
    seq = k_ref.shape[1]
    grp = SWA_HEADS // SWA_KV_HEADS
    i = pl.program_id(1)
    start = pl.multiple_of(jnp.clip((i - 1) * blk, 0, seq - win), blk)
    lo = _first_head_lanes()
    kpos = start + lax.broadcasted_iota(jnp.int32, (win, blk), 0)
    qpos = i * blk + lax.broadcasted_iota(jnp.int32, (win, blk), 1)
    valid = jnp.abs(kpos - qpos) <= SWA_WINDOW
    valid = jnp.concatenate([valid] * grp, axis=1)
    for j in range(SWA_KV_HEADS):
        sl = slice(j * PAIR, (j + 1) * PAIR)
        outs = _swa_group(q_ref, j, k_ref[0, pl.ds(start, win), sl], v_ref[0, pl.ds(start, win), sl], lo,
                          [kc_ref[0, :, sl]], [vc_ref[0, :, sl]], sink_ref, valid)
        for pp, o in enumerate(outs):
            o_ref[0, :, (j * grp // 2 + pp) * PAIR:(j * grp // 2 + pp + 1) * PAIR] = o.astype(o_ref.dtype)


def _swa(sink, p, pc):
    bsz, seq, w = p["swa_q"].shape
    t = pc["swa_k"].shape[1]
    kvw = SWA_KV_HEADS * PAIR
    kv = pl.BlockSpec((1, seq, kvw), lambda b, i: (b, 0, 0))
    kvc = pl.BlockSpec((1, t, kvw), lambda b, i: (b, 0, 0))
    qb = pl.BlockSpec((1, SWA_BLOCK, w), lambda b, i: (b, i, 0))
    return pl.pallas_call(
        _swa_kernel,
        grid=(bsz, seq // SWA_BLOCK),
        in_specs=[pl.BlockSpec(memory_space=pltpu.SMEM), qb, kv, kv, kvc, kvc],
        out_specs=qb,
        out_shape=jax.ShapeDtypeStruct((bsz, seq, w), BF16),
        compiler_params=_params(("parallel", "parallel"), 32),
        name="swa",
    )(sink, p["swa_q"], p["swa_k"], p["swa_v"], pc["swa_k"], pc["swa_v"])


NA_GROUP = 4
NA_BAND = 12
NA_GRID_ROWS = 32
_NA_CFGS = ((0, 0), (4, 0), (8, 4), (28, 20))


def _na_bias_kernel(rpb_ref, o_ref):
    hp = pl.program_id(0)
    gw, wc = GRID_W, NA_COLS
    rows = NA_GRID_ROWS
    ck = lax.broadcasted_iota(jnp.int32, (gw, gw), 0)
    cq = lax.broadcasted_iota(jnp.int32, (gw, gw), 1)
    col_start = jnp.clip(cq - wc // 2, 0, gw - wc)
    col_ok = (ck >= col_start) & (ck < col_start + wc)
    dc = jnp.clip(ck - cq, -(wc - 1), wc - 1) + (wc - 1)
    neg = jnp.full((gw, gw), NEG_INF, F32)
    per_dr = []
    for j in range(2):
        tiles = []
        for dr in range(2 * NA_ROWS - 1):
            t = jnp.zeros((gw, gw), F32)
            for d in range(2 * wc - 1):
                t = jnp.where(dc == d, rpb_ref[2 * hp + j, dr, d], t)
            tiles.append(jnp.where(col_ok, t, NEG_INF))
        per_dr.append(tiles)
    for c, (r0, start) in enumerate(_NA_CFGS):
        for jp in range(NA_BAND):
            krow = start + jp
            blocks = []
            for j in range(2):
                for i in range(NA_GROUP):
                    r = r0 + i
                    band0 = min(max(r - NA_ROWS // 2, 0), rows - NA_ROWS)
                    blocks.append(per_dr[j][krow - r + NA_ROWS - 1] if band0 <= krow < band0 + NA_ROWS else neg)
            o_ref[c, 0, jp * gw:(jp + 1) * gw, :] = jnp.concatenate(blocks, axis=1)


def _na_bias(rpb):
    n_cfg = len(_NA_CFGS)
    blk = (n_cfg, 1, NA_BAND * GRID_W, 2 * NA_GROUP * GRID_W)
    return pl.pallas_call(
        _na_bias_kernel,
        grid=(NA_HEADS // 2,),
        in_specs=[pl.BlockSpec(memory_space=pltpu.SMEM)],
        out_specs=pl.BlockSpec(blk, lambda h: (0, h, 0, 0)),
        out_shape=jax.ShapeDtypeStruct((n_cfg, NA_HEADS // 2) + blk[2:], F32),
        compiler_params=_params(("parallel",), 32),
        name="na_bias",
    )(rpb)


def _na_pair(qp, lo, keys, values, bias):
    m = qp.shape[0]
    q2 = jnp.concatenate([_keep_head(qp, lo, 0), _keep_head(qp, lo, 1)], axis=0)
    scores = [_dot_nt(k, q2) for k in keys]
    if bias is not None:
        scores[0] = scores[0] + bias
    o = _softmax_pv_t(scores, values)
    return jnp.where(lo, o[:m], o[m:])


def _na_kernel(q_ref, k_ref, v_ref, kc_ref, vc_ref, bias_ref, o_ref):
    rows = k_ref.shape[1] // GRID_W
    nk = NA_BAND * GRID_W
    g = pl.program_id(1)
    first = jnp.clip(g * NA_GROUP - NA_ROWS // 2, 0, rows - NA_BAND)
    st = pl.multiple_of(first * GRID_W, GRID_W)
    lo = _first_head_lanes()
    for hp in range(NA_HEADS // 2):
        sl = slice(hp * PAIR, (hp + 1) * PAIR)
        o = _na_pair(q_ref[0, :, sl], lo, [k_ref[0, pl.ds(st, nk), sl], kc_ref[0, :, sl]],
                     [v_ref[0, pl.ds(st, nk), sl], vc_ref[0, :, sl]], bias_ref[0, hp])
        o_ref[0, :, sl] = o.astype(o_ref.dtype)


def _na(bias, p, pc):
    bsz, seq, w = p["na_q"].shape
    t = pc["na_k"].shape[1]
    rows = seq // GRID_W
    assert rows == NA_GRID_ROWS and bias.shape[0] == len(_NA_CFGS)
    n_groups = rows // NA_GROUP
    kv = pl.BlockSpec((1, seq, w), lambda b, g: (b, 0, 0))
    kvc = pl.BlockSpec((1, t, w), lambda b, g: (b, 0, 0))
    qb = pl.BlockSpec((1, NA_GROUP * GRID_W, w), lambda b, g: (b, g, 0))

    def cfg(b, g):
        return (jnp.where(g < 2, g, jnp.where(g < n_groups - 1, 2, 3)), 0, 0, 0)

    return pl.pallas_call(
        _na_kernel,
        grid=(bsz, n_groups),
        in_specs=[qb, kv, kv, kvc, kvc,
                  pl.BlockSpec((1, NA_HEADS // 2, NA_BAND * GRID_W, 2 * NA_GROUP * GRID_W), cfg)],
        out_specs=qb,
        out_shape=jax.ShapeDtypeStruct((bsz, seq, w), BF16),
        compiler_params=_params(("parallel", "arbitrary"), 48),
        name="na",
    )(p["na_q"], p["na_k"], p["na_v"], pc["na_k"], pc["na_v"], bias)


def _ctx_attn_kernel(sink_ref, sq_ref, sk_ref, sv_ref, nq_ref, nk_ref, nv_ref, so_ref, no_ref):
    grp = SWA_HEADS // SWA_KV_HEADS
    lo = _first_head_lanes()
    for j in range(SWA_KV_HEADS):
        sl = slice(j * PAIR, (j + 1) * PAIR)
        outs = _swa_group(sq_ref, j, None, None, lo, [sk_ref[0, :, sl]], [sv_ref[0, :, sl]], sink_ref, None)
        for pp, o in enumerate(outs):
            so_ref[0, :, (j * grp // 2 + pp) * PAIR:(j * grp // 2 + pp + 1) * PAIR] = o.astype(so_ref.dtype)
    for hp in range(NA_HEADS // 2):
        sl = slice(hp * PAIR, (hp + 1) * PAIR)
        o = _na_pair(nq_ref[0, :, sl], lo, [nk_ref[0, :, sl]], [nv_ref[0, :, sl]], None)
        no_ref[0, :, sl] = o.astype(no_ref.dtype)


def _ctx_attn(sink, pc):
    bsz, t, w = pc["swa_q"].shape
    kvw = SWA_KV_HEADS * PAIR
    full = pl.BlockSpec((1, t, w), lambda b: (b, 0, 0))
    kv = pl.BlockSpec((1, t, kvw), lambda b: (b, 0, 0))
    return pl.pallas_call(
        _ctx_attn_kernel,
        grid=(bsz,),
        in_specs=[pl.BlockSpec(memory_space=pltpu.SMEM), full, kv, kv, full, full, full],
        out_specs=[full, full],
        out_shape=[jax.ShapeDtypeStruct((bsz, t, w), BF16)] * 2,
        compiler_params=_params(("parallel",), 32),
        name="ctx_attn",
    )(sink, pc["swa_q"], pc["swa_k"], pc["swa_v"], pc["na_q"], pc["na_k"], pc["na_v"])


def _outproj_kernel(yr_ref, yc_ref, ys_ref, yn_ref, w_ref, x_ref, g1_ref, gain_ref, sc_ref, sh_ref, wr_ref,
                    xo_ref, h_ref, lg_ref):
    gw = GROUP_WIDTH
    acc = _dot(yr_ref[0], w_ref[0:gw, :])
    acc += _dot(yc_ref[0], w_ref[gw:2 * gw, :])
    acc += _dot(ys_ref[0], w_ref[2 * gw:3 * gw, :])
    acc += _dot(yn_ref[0], w_ref[3 * gw:4 * gw, :])
    xn = x_ref[0] + g1_ref[0] * acc
    xo_ref[0] = xn
    hb = _rms_modulate(xn, gain_ref[...], sc_ref[0], sh_ref[0]).astype(BF16)
    h_ref[0] = hb
    lg_ref[0] = _dot(hb, wr_ref[...])


def _outproj(ys, w_out, x, g1, gain, sc, sh, w_router, *, tm):
    bsz, seq, d = x.shape
    yb = pl.BlockSpec((1, tm, GROUP_WIDTH), lambda b, i: (b, i, 0))
    xb = pl.BlockSpec((1, tm, d), lambda b, i: (b, i, 0))
    vec = pl.BlockSpec((1, 1, d), lambda b, i: (b, 0, 0))
    return pl.pallas_call(
        _outproj_kernel,
        grid=(bsz, seq // tm),
        in_specs=[yb, yb, yb, yb,
                  pl.BlockSpec((d, d), lambda b, i: (0, 0), pipeline_mode=pl.Buffered(1)),
                  xb, vec, pl.BlockSpec((1, d), lambda b, i: (0, 0)), vec, vec,
                  pl.BlockSpec((d, N_EXPERTS), lambda b, i: (0, 0))],
        out_specs=[xb, xb, pl.BlockSpec((1, tm, N_EXPERTS), lambda b, i: (b, i, 0))],
        out_shape=[jax.ShapeDtypeStruct((bsz, seq, d), F32), jax.ShapeDtypeStruct((bsz, seq, d), BF16),
                   jax.ShapeDtypeStruct((bsz, seq, N_EXPERTS), F32)],
        compiler_params=_params(("parallel", "parallel"), 48),
        name="outproj",
    )(*ys, w_out, x, g1, gain.reshape(1, d), sc, sh, w_router.astype(BF16))


def _cumsum_lanes(x, tri):
    outs = []
    carry = jnp.zeros((x.shape[0], 1), F32)
    for c0 in range(0, x.shape[1], 128):
        cs = _dot(x[:, c0:c0 + 128], tri) + carry
        outs.append(cs)
        carry = cs[:, 127:128]
    return jnp.concatenate(outs, axis=1) if len(outs) > 1 else outs[0]


def _router_kernel(lg_ref, pos_ref, gate_ref, *, cap):
    lg = lg_ref[0]
    e = jnp.exp(lg - jnp.max(lg, axis=0, keepdims=True))
    aff = e / jnp.sum(e, axis=0, keepdims=True)
    capf = float(cap)
    lo = jnp.zeros((lg.shape[0], 1), jnp.int32)
    for bit in range(30, -1, -1):
        cand = lo | (1 << bit)
        cnt = jnp.sum(jnp.where(aff >= pltpu.bitcast(cand, F32), 1.0, 0.0), axis=1, keepdims=True)
        lo = jnp.where(cnt >= capf, cand, lo)
    kth = pltpu.bitcast(lo, F32)
    gt = aff > kth
    eq = aff == kth
    need = capf - jnp.sum(jnp.where(gt, 1.0, 0.0), axis=1, keepdims=True)
    ri = lax.broadcasted_iota(jnp.int32, (128, 128), 0)
    ci = lax.broadcasted_iota(jnp.int32, (128, 128), 1)
    tri = jnp.where(ri <= ci, 1.0, 0.0).astype(BF16)
    eq_rank = _cumsum_lanes(jnp.where(eq, 1.0, 0.0).astype(BF16), tri)
    sel = gt | (eq & (eq_rank <= need))
    slot = _cumsum_lanes(jnp.where(sel, 1.0, 0.0).astype(BF16), tri) - 1.0
    pos_ref[0] = jnp.where(sel, slot, -1.0).astype(jnp.int32)
    gate_ref[0] = aff


def _router(logits_t, cap):
    bsz, n_e, t = logits_t.shape
    blk = pl.BlockSpec((1, n_e, t), lambda b: (b, 0, 0))
    return pl.pallas_call(
        functools.partial(_router_kernel, cap=cap),
        grid=(bsz,),
        in_specs=[blk],
        out_specs=[blk, blk],
        out_shape=[jax.ShapeDtypeStruct((bsz, n_e, t), jnp.int32), jax.ShapeDtypeStruct((bsz, n_e, t), F32)],
        compiler_params=_params(("parallel",), 32),
        name="router",
    )(logits_t)


EXPERT_ROWS = 256


def _expert_kernel(pos_ref, gate_ref, h_ref, wg_ref, wu_ref, wd_ref, o_ref, *, cap):
    nb, t = h_ref.shape[0], h_ref.shape[1]
    slot = lax.broadcasted_iota(jnp.int32, (cap, t), 0)
    rows, gates = [], []
    for i in range(nb):
        hit = slot == pos_ref[i, 0]
        rows.append(_dot(jnp.where(hit, 1.0, 0.0).astype(BF16), h_ref[i]).astype(BF16))
        gates.append(jnp.sum(jnp.where(hit, gate_ref[i, 0], 0.0), axis=1, keepdims=True))
    xs = jnp.concatenate(rows, axis=0) if nb > 1 else rows[0]
    gate = jnp.concatenate(gates, axis=0) if nb > 1 else gates[0]
    a = _dot(xs, wg_ref[...])
    u = _dot(xs, wu_ref[...])
    ye = (_dot((_silu(a) * u).astype(BF16), wd_ref[...]) * gate).astype(o_ref.dtype)
    for i in range(nb):
        o_ref[i, 0] = ye[i * cap:(i + 1) * cap]


def _experts(pos, gate, h, wg, wu, wd, layer, cap):
    bsz, t, d = h.shape
    _, n_e, _, ff = wg.shape
    nb = min(bsz, max(1, EXPERT_ROWS // cap))
    assert bsz % nb == 0
    row = pl.BlockSpec((nb, 1, 1, t), lambda e, b: (b, e, 0, 0))
    return pl.pallas_call(
        functools.partial(_expert_kernel, cap=cap),
        grid=(n_e, bsz // nb),
        in_specs=[row, row, pl.BlockSpec((nb, t, d), lambda e, b: (b, 0, 0)),
                  pl.BlockSpec((None, None, d, ff), lambda e, b: (layer, e, 0, 0)),
                  pl.BlockSpec((None, None, d, ff), lambda e, b: (layer, e, 0, 0)),
                  pl.BlockSpec((None, None, ff, d), lambda e, b: (layer, e, 0, 0))],
        out_specs=pl.BlockSpec((nb, 1, cap, d), lambda e, b: (b, e, 0, 0)),
        out_shape=jax.ShapeDtypeStruct((bsz, n_e, cap, d), BF16),
        compiler_params=_params(("arbitrary", "arbitrary"), 56),
        name="experts",
    )(pos.reshape(bsz, n_e, 1, t), gate.reshape(bsz, n_e, 1, t), h, wg, wu, wd)


def _combine_kernel(pos_ref, ye_ref, x_ref, g2_ref, *rest, cap):
    o_ref = rest[-1]
    n_e = pos_ref.shape[1]
    tb = x_ref.shape[1]
    slot = lax.broadcasted_iota(jnp.int32, (cap, tb), 0)
    hit = jnp.concatenate([jnp.where(slot == pos_ref[0, e:e + 1, :], 1.0, 0.0).astype(BF16) for e in range(n_e)],
                          axis=0)
    xn = x_ref[0] + g2_ref[0] * _dot_tn(hit, ye_ref[0])
    if len(rest) == 2:
        ms = jnp.mean(xn * xn, axis=-1, keepdims=True)
        xn = (xn * lax.rsqrt(ms + EPS)) * rest[0][...]
    o_ref[0] = xn


def _combine(pos, ye, x, g2, cap, tb, final_gain=None):
    bsz, t, d = x.shape
    n_e = pos.shape[1]
    xb = pl.BlockSpec((1, tb, d), lambda b, i: (b, i, 0))
    in_specs = [pl.BlockSpec((1, n_e, tb), lambda b, i: (b, 0, i)),
                pl.BlockSpec((1, n_e * cap, d), lambda b, i: (b, 0, 0)),
                xb, pl.BlockSpec((1, 1, d), lambda b, i: (b, 0, 0))]
    args = [pos, ye.reshape(bsz, n_e * cap, d), x, g2]
    if final_gain is not None:
        in_specs.append(pl.BlockSpec((1, d), lambda b, i: (0, 0)))
        args.append(final_gain.reshape(1, d))
    return pl.pallas_call(
        functools.partial(_combine_kernel, cap=cap),
        grid=(bsz, t // tb),
        in_specs=in_specs,
        out_specs=xb,
        out_shape=jax.ShapeDtypeStruct((bsz, t, d), F32),
        compiler_params=_params(("parallel", "arbitrary"), 56),
        name="combine",
    )(*args)


def _moe(x, h, logits, g2, lw, final_gain=None):
    bsz, t, d = x.shape
    cap = EC_CAPACITY_FACTOR * t // N_EXPERTS
    pos, gate = _router(jnp.swapaxes(logits, 1, 2), cap)
    ye = _experts(pos, gate, h, lw["w_gate"], lw["w_up"], lw["w_down"], lw["layer"], cap)
    return _combine(pos, ye, x, g2, cap, min(t, 512), final_gain)


def _layer(x, xc, mods, mods_c, lw, need_ctx, final_gain):
    sh1, sc1, g1, sh2, sc2, g2 = mods
    sh1c, sc1c, g1c, sh2c, sc2c, g2c = mods_c
    names_lat = tuple(p[0] for p in _PIECES)
    p = _inproj(x, lw["norm_mix"], sc1, sh1, lw["w_in"], rope=True, names=names_lat, tm=256)
    pc = _inproj(xc, lw["norm_mix"], sc1c, sh1c, lw["w_in"], rope=False,
                 names=names_lat if need_ctx else _CTX_KV_ONLY, tm=256)

    y_ret, yc_ret = _retention(lw["ret_decay"], p, pc, need_ctx)
    y_conv = _short_conv(lw["conv_w"], p["conv_b"], p["conv_c"], p["conv_h"])
    y_swa = _swa(lw["swa_sink"], p, pc)
    y_na = _na(_na_bias(lw["na_rpb"]), p, pc)
    x, h, logits = _outproj((y_ret, y_conv, y_swa, y_na), lw["w_out"], x, g1, lw["norm_ffn"], sc2, sh2,
                            lw["w_router"], tm=256)
    x = _moe(x, h, logits, g2, lw, final_gain)
    if need_ctx:
        yc_conv = _short_conv(lw["conv_w"], pc["conv_b"], pc["conv_c"], pc["conv_h"])
        yc_swa, yc_na = _ctx_attn(lw["swa_sink"], pc)
        xc, hc, logits_c = _outproj((yc_ret, yc_conv, yc_swa, yc_na), lw["w_out"], xc, g1c, lw["norm_ffn"],
                                    sc2c, sh2c, lw["w_router"], tm=256)
        xc = _moe(xc, hc, logits_c, g2c, lw)
    return x, xc


def kernel(x, c, ctx, c_ctx, w_ada, b_ada, norm_mix, norm_ffn, w_in, w_out, ret_decay_fwd, ret_decay_bwd,
           conv_w, swa_sink, na_rpb, w_router, w_gate, w_up, w_down, norm_final):
    bsz, _, d = x.shape
    depth = w_ada.shape[0]
    assert bsz + 1 <= ADA_ROWS
    cs = jnp.zeros((ADA_ROWS, d), F32).at[:bsz].set(c).at[bsz].set(c_ctx)
    ada = _ada(cs, w_ada, b_ada)
    w_out_b = w_out.astype(BF16)
    wg_b, wu_b, wd_b = w_gate.astype(BF16), w_up.astype(BF16), w_down.astype(BF16)
    xc = ctx
    for l in range(depth):
        mods = tuple(ada[l, :bsz, i * d:(i + 1) * d].reshape(bsz, 1, d) for i in range(6))
        mods_c = tuple(jnp.broadcast_to(ada[l, bsz, i * d:(i + 1) * d].reshape(1, 1, d), (bsz, 1, d))
                       for i in range(6))
        lw = dict(norm_mix=norm_mix[l], norm_ffn=norm_ffn[l], w_in=_widen_swa_kv(w_in[l].astype(BF16)),
                  w_out=w_out_b[l], ret_decay=jnp.stack([ret_decay_fwd[l], ret_decay_bwd[l]]), conv_w=conv_w[l],
                  swa_sink=swa_sink[l], na_rpb=na_rpb[l], w_router=w_router[l],
                  w_gate=wg_b, w_up=wu_b, w_down=wd_b, layer=l)
        last = l == depth - 1
        x, xc = _layer(x, xc, mods, mods_c, lw, need_ctx=not last, final_gain=norm_final if last else None)
    return x
```

```python
import functools

import numpy as np
import jax
import jax.numpy as jnp
from jax import lax
from jax.experimental import pallas as pl
from jax.experimental.pallas import tpu as pltpu

D_MODEL = 2048
DEPTH = 2
GRID_W = 64
HEAD_DIM = 64
GROUP_WIDTH = D_MODEL // 4
RET_CHUNK = 128
SWA_HEADS = 8
SWA_KV_HEADS = 2
SWA_WINDOW = 128
SWA_BLOCK = 128
NA_HEADS = 8
NA_ROWS = 8
NA_COLS = 16
N_EXPERTS = 16
EXPERT_FF = D_MODEL // 2
EC_CAPACITY_FACTOR = 2
ROPE_BASE = 10000.0
EPS = 1e-6
NEG_INF = -1e30
F32 = jnp.float32
BF16 = jnp.bfloat16
ADA_ROWS = 16
QK_SCALE = HEAD_DIM ** -0.5
PAIR = 2 * HEAD_DIM

_PIECES = (
    ("ret_q", 0, 512, True, 1.0),
    ("ret_k", 512, 512, True, QK_SCALE),
    ("ret_v", 1024, 512, False, 1.0),
    ("ret_g", 1536, 512, False, 1.0),
    ("conv_b", 2048, 512, False, 1.0),
    ("conv_c", 2560, 512, False, 1.0),
    ("conv_h", 3072, 512, False, 1.0),
    ("swa_q", 3584, 512, True, QK_SCALE),
    ("swa_k", 4096, 256, True, 1.0),
    ("swa_v", 4352, 256, False, 1.0),
    ("na_q", 4608, 512, False, QK_SCALE),
    ("na_k", 5120, 512, False, 1.0),
    ("na_v", 5632, 512, False, 1.0),
)
IN_COLS = 6144
_SWA_KV_COL = 4096
_CTX_KV_ONLY = ("ret_k", "ret_v", "swa_k", "swa_v", "na_k", "na_v")


def _widen_swa_kv(w):
    hd, c0 = HEAD_DIM, _SWA_KV_COL
    parts = [w[:, :c0]]
    for base in (c0, c0 + SWA_KV_HEADS * hd):
        for j in range(SWA_KV_HEADS):
            head = w[:, base + j * hd:base + (j + 1) * hd]
            parts += [head, head]
    parts.append(w[:, c0 + 2 * SWA_KV_HEADS * hd:])
    return jnp.concatenate(parts, axis=1)


def _params(sem, vmem_mb):
    return pltpu.CompilerParams(dimension_semantics=sem, vmem_limit_bytes=vmem_mb << 20)


def _silu(x):
    return x * jax.nn.sigmoid(x)


def _dot(a, b):
    return jnp.dot(a, b, preferred_element_type=F32)


def _dot_nt(a, b):
    return lax.dot_general(a, b, (((1,), (1,)), ((), ())), preferred_element_type=F32)


def _dot_tn(a, b):
    return lax.dot_general(a, b, (((0,), (0,)), ((), ())), preferred_element_type=F32)


def _first_head_lanes():
    return lax.broadcasted_iota(jnp.int32, (1, PAIR), 1) < HEAD_DIM


def _keep_head(x, lo, j):
    return jnp.where(lo if j == 0 else jnp.logical_not(lo), x, jnp.zeros_like(x))


def _ada_kernel(c_ref, w_ref, b_ref, o_ref):
    s = _silu(c_ref[...]).astype(BF16)
    o_ref[0] = _dot(s, w_ref[0].astype(BF16)) + b_ref[0]


def _ada(cs, w_ada, b_ada):
    depth, d, n = w_ada.shape
    tn = 1024
    return pl.pallas_call(
        _ada_kernel,
        grid=(depth, n // tn),
        in_specs=[
            pl.BlockSpec((ADA_ROWS, d), lambda l, j: (0, 0)),
            pl.BlockSpec((1, d, tn), lambda l, j: (l, 0, j)),
            pl.BlockSpec((1, 1, tn), lambda l, j: (l, 0, j)),
        ],
        out_specs=pl.BlockSpec((1, ADA_ROWS, tn), lambda l, j: (l, 0, j)),
        out_shape=jax.ShapeDtypeStruct((depth, ADA_ROWS, n), F32),
        compiler_params=_params(("arbitrary", "arbitrary"), 40),
        name="ada",
    )(cs, w_ada, b_ada.reshape(depth, 1, n))


def _rope_tables(seq):
    half = HEAD_DIM // 2
    nf = half // 2
    t = np.arange(seq)
    row = (t // GRID_W).astype(np.float32)
    col = (t % GRID_W).astype(np.float32)
    inv = (np.float32(ROPE_BASE) ** (-np.arange(nf, dtype=np.float32) / np.float32(nf))).astype(np.float32)
    ang = np.concatenate([row[:, None] * inv, col[:, None] * inv], axis=-1).astype(np.float32)
    cos, sin = np.cos(ang), np.sin(ang)
    cos64 = np.concatenate([cos, cos], axis=-1)
    sin64 = np.concatenate([-sin, sin], axis=-1)
    return (np.tile(cos64, (1, 2)).astype(np.float32), np.tile(sin64, (1, 2)).astype(np.float32))


def _rope(z, cos, sin_signed):
    w = z.shape[1]
    reps = w // 128
    cos_w = jnp.concatenate([cos] * reps, axis=1) if reps > 1 else cos
    sin_w = jnp.concatenate([sin_signed] * reps, axis=1) if reps > 1 else sin_signed
    lane = lax.broadcasted_iota(jnp.int32, (1, w), 1)
    first_half = (lane & (HEAD_DIM - 1)) < (HEAD_DIM // 2)
    up = pltpu.roll(z, w - HEAD_DIM // 2, 1)
    dn = pltpu.roll(z, HEAD_DIM // 2, 1)
    return z * cos_w + jnp.where(first_half, up, dn) * sin_w


def _rms_modulate(x, gain, scale, shift):
    ms = jnp.mean(x * x, axis=-1, keepdims=True)
    return (x * lax.rsqrt(ms + EPS)) * (gain * (1.0 + scale)) + shift


def _inproj_kernel(*refs, pieces, rope):
    if rope:
        x_ref, gain_ref, sc_ref, sh_ref, cos_ref, sin_ref, w_ref = refs[:7]
        o_refs = refs[7:]
    else:
        x_ref, gain_ref, sc_ref, sh_ref, w_ref = refs[:5]
        o_refs = refs[5:]
    hb = _rms_modulate(x_ref[0], gain_ref[...], sc_ref[0], sh_ref[0]).astype(BF16)
    for (_, off, width, do_rope, scale), o_ref in zip(pieces, o_refs):
        z = _dot(hb, w_ref[:, off:off + width])
        if rope and do_rope:
            z = _rope(z, cos_ref[...], sin_ref[...])
        if scale != 1.0:
            z = z * scale
        o_ref[0] = z.astype(BF16)


def _inproj(x, gain, sc, sh, w_in, *, rope, names, tm):
    bsz, seq, d = x.shape
    pieces = tuple(p for p in _PIECES if p[0] in names)
    vec = pl.BlockSpec((1, 1, d), lambda b, i: (b, 0, 0))
    in_specs = [pl.BlockSpec((1, tm, d), lambda b, i: (b, i, 0)),
                pl.BlockSpec((1, d), lambda b, i: (0, 0)), vec, vec]
    args = [x, gain.reshape(1, d), sc, sh]
    if rope:
        cos, sin = _rope_tables(seq)
        in_specs += [pl.BlockSpec((tm, 128), lambda b, i: (i, 0))] * 2
        args += [jnp.asarray(cos), jnp.asarray(sin)]
    in_specs.append(pl.BlockSpec((d, IN_COLS), lambda b, i: (0, 0), pipeline_mode=pl.Buffered(1)))
    args.append(w_in)
    outs = pl.pallas_call(
        functools.partial(_inproj_kernel, pieces=pieces, rope=rope),
        grid=(bsz, seq // tm),
        in_specs=in_specs,
        out_specs=[pl.BlockSpec((1, tm, p[2]), lambda b, i: (b, i, 0)) for p in pieces],
        out_shape=[jax.ShapeDtypeStruct((bsz, seq, p[2]), BF16) for p in pieces],
        compiler_params=_params(("parallel", "parallel"), 56),
        name="inproj_lat" if rope else "inproj_ctx",
    )(*args)
    return {p[0]: o for p, o in zip(pieces, outs)}


def _log_sigmoid(x):
    return jnp.minimum(x, 0.0) - jnp.log(1.0 + jnp.exp(-jnp.abs(x)))


RET_PAIRS = 2


def _ret_kernel(dec_ref, q_ref, k_ref, v_ref, g_ref, qc_ref, kc_ref, vc_ref, gc_ref, *rest, need_ctx):
    if need_ctx:
        y_ref, yc_ref, sb_ref = rest
    else:
        (y_ref, sb_ref), yc_ref = rest, None
    c = RET_CHUNK
    hd = HEAD_DIM
    n_lat = q_ref.shape[1] // c
    n_ctx = qc_ref.shape[1] // c
    n_pairs = q_ref.shape[2] // PAIR
    first_head = 2 * n_pairs * pl.program_id(1)

    lo = _first_head_lanes()
    ri = lax.broadcasted_iota(jnp.int32, (c, c), 0)
    ci = lax.broadcasted_iota(jnp.int32, (c, c), 1)
    diff = (ri - ci).astype(F32)
    same_head = (ri < hd) == (ci < hd)
    pos = lax.broadcasted_iota(jnp.int32, (c, PAIR), 0).astype(F32)

    consts = []
    for p in range(n_pairs):
        lg_f = [_log_sigmoid(jnp.full((1, c), dec_ref[0, first_head + 2 * p + j], F32)) for j in range(2)]
        lg_b = [_log_sigmoid(jnp.full((1, c), dec_ref[1, first_head + 2 * p + j], F32)) for j in range(2)]
        dmat = jnp.concatenate(
            [jnp.where(diff >= 0.0, jnp.exp(lg_f[j] * jnp.maximum(diff, 0.0)),
                       jnp.exp(lg_b[j] * jnp.maximum(-diff, 0.0))) for j in range(2)], axis=0)
        lf = jnp.where(lo, lg_f[0], lg_f[1])
        lb = jnp.where(lo, lg_b[0], lg_b[1])
        consts.append(dict(dmat=dmat, xi_f=jnp.exp(lf * (pos + 1.0)), xi_b=jnp.exp(lb * (c - pos)),
                           ze_f=jnp.exp(lf * (c - 1.0 - pos)), ze_b=jnp.exp(lb * pos),
                           cd_f=jnp.exp(lf * float(c)), cd_b=jnp.exp(lb * float(c))))

    def rows(ref, r0, p):
        return ref[0, pl.ds(r0, c), p * PAIR:(p + 1) * PAIR]

    def kv_outer(k, v, zeta):
        kz = (k.astype(F32) * zeta).astype(BF16)
        return jnp.where(same_head, _dot_tn(kz, v), 0.0)

    def half_sum(x):
        s_lo = jnp.sum(jnp.where(lo, x, 0.0), axis=-1, keepdims=True)
        s_all = jnp.sum(x, axis=-1, keepdims=True)
        return jnp.where(lo, s_lo, s_all - s_lo)

    def bwd_chunk(k_r, v_r, r0, slot, states):
        new = []
        for p in range(n_pairs):
            sb_ref[slot, p] = states[p]
            new.append(states[p] * consts[p]["cd_b"] + kv_outer(rows(k_r, r0, p), rows(v_r, r0, p), consts[p]["ze_b"]))
        return tuple(new)

    def fwd_chunk(q_r, k_r, v_r, g_r, o_r, r0, slot, states):
        new = []
        for p in range(n_pairs):
            cp = consts[p]
            k, v = rows(k_r, r0, p), rows(v_r, r0, p)
            if o_r is not None:
                q = rows(q_r, r0, p)
                q2 = jnp.concatenate([_keep_head(q, lo, 0), _keep_head(q, lo, 1)], axis=0)
                a = (_dot_nt(q2, k) * cp["dmat"]).astype(BF16)
                qf = q.astype(F32)
                lhs = jnp.concatenate([a[:c], a[c:], (qf * cp["xi_f"]).astype(BF16), (qf * cp["xi_b"]).astype(BF16)],
                                      axis=1)
                rhs = jnp.concatenate([_keep_head(v, lo, 0), _keep_head(v, lo, 1), states[p].astype(BF16),
                                       sb_ref[slot, p].astype(BF16)], axis=0)
                y = _dot(lhs, rhs)
                yc = y - half_sum(y) * (1.0 / hd)
                var = half_sum(yc * yc) * (1.0 / hd)
                out = _silu(rows(g_r, r0, p).astype(F32)) * (yc * lax.rsqrt(var + EPS))
                o_r[0, pl.ds(r0, c), p * PAIR:(p + 1) * PAIR] = out.astype(o_r.dtype)
            new.append(states[p] * cp["cd_f"] + kv_outer(k, v, cp["ze_f"]))
        return tuple(new)

    zero = tuple(jnp.zeros((PAIR, PAIR), F32) for _ in range(n_pairs))
    st = zero
    for cc in range(n_ctx - 1, -1, -1):
        st = bwd_chunk(kc_ref, vc_ref, cc * c, cc, st)

    def bwd_body(i, s):
        cl = n_lat - 1 - i
        return bwd_chunk(k_ref, v_ref, pl.multiple_of(cl * c, c), n_ctx + cl, s)

    lax.fori_loop(0, n_lat, bwd_body, st, unroll=4)

    st = zero
    for cc in range(n_ctx):
        st = fwd_chunk(qc_ref, kc_ref, vc_ref, gc_ref, yc_ref, cc * c, cc, st)

    def fwd_body(cl, s):
        return fwd_chunk(q_ref, k_ref, v_ref, g_ref, y_ref, pl.multiple_of(cl * c, c), n_ctx + cl, s)

    lax.fori_loop(0, n_lat, fwd_body, st, unroll=4)


def _retention(dec, p, pc, need_ctx):
    bsz, seq, w = p["ret_q"].shape
    t = pc["ret_k"].shape[1]
    bw = RET_PAIRS * PAIR
    lat = pl.BlockSpec((1, seq, bw), lambda b, h: (b, 0, h))
    ctx = pl.BlockSpec((1, t, bw), lambda b, h: (b, 0, h))
    out_specs, out_shape = [lat], [jax.ShapeDtypeStruct((bsz, seq, w), BF16)]
    if need_ctx:
        out_specs.append(ctx)
        out_shape.append(jax.ShapeDtypeStruct((bsz, t, w), BF16))
        qc, gc = pc["ret_q"], pc["ret_g"]
    else:
        qc, gc = pc["ret_k"], pc["ret_k"]
    n_chunks = (seq + t) // RET_CHUNK
    outs = pl.pallas_call(
        functools.partial(_ret_kernel, need_ctx=need_ctx),
        grid=(bsz, w // bw),
        in_specs=[pl.BlockSpec(memory_space=pltpu.SMEM), lat, lat, lat, lat, ctx, ctx, ctx, ctx],
        out_specs=out_specs,
        out_shape=out_shape,
        scratch_shapes=[pltpu.VMEM((n_chunks, RET_PAIRS, PAIR, PAIR), F32)],
        compiler_params=_params(("parallel", "parallel"), 32),
        name="retention",
    )(dec, p["ret_q"], p["ret_k"], p["ret_v"], p["ret_g"], qc, pc["ret_k"], pc["ret_v"], gc)
    return (outs[0], outs[1]) if need_ctx else (outs[0], None)


def _conv_kernel(w_ref, b_ref, c_ref, h_ref, o_ref):
    seq = b_ref.shape[1]
    rows = min(256, seq)
    halo = 16
    w0, w1, w2 = w_ref[0:1, :], w_ref[1:2, :], w_ref[2:3, :]
    for r0 in range(0, seq, rows):
        lo, hi = max(0, r0 - halo), min(seq, r0 + rows + halo)
        u = c_ref[0, lo:hi, :].astype(F32) * h_ref[0, lo:hi, :].astype(F32)
        n = hi - lo
        t = lo + lax.broadcasted_iota(jnp.int32, (n, 1), 0)
        prev = jnp.where(t == 0, 0.0, pltpu.roll(u, 1, 0))
        nxt = jnp.where(t == seq - 1, 0.0, pltpu.roll(u, n - 1, 0))
        y = prev * w0 + u * w1 + nxt * w2
        y = y[r0 - lo:r0 - lo + rows]
        o_ref[0, r0:r0 + rows, :] = (b_ref[0, r0:r0 + rows, :].astype(F32) * y).astype(o_ref.dtype)


def _short_conv(conv_w, b, c, h):
    bsz, seq, w = b.shape
    blk = pl.BlockSpec((1, seq, w), lambda i: (i, 0, 0))
    return pl.pallas_call(
        _conv_kernel,
        grid=(bsz,),
        in_specs=[pl.BlockSpec((3, w), lambda i: (0, 0)), blk, blk, blk],
        out_specs=blk,
        out_shape=jax.ShapeDtypeStruct((bsz, seq, w), BF16),
        compiler_params=_params(("parallel",), 48),
        name="short_conv",
    )(conv_w, b, c, h)


def _softmax_pv_t(scores_t, values, extra=None):
    mx = functools.reduce(jnp.maximum, [jnp.max(s, axis=0, keepdims=True) for s in scores_t])
    if extra is not None:
        mx = jnp.maximum(mx, extra)
    den = jnp.exp(extra - mx) if extra is not None else 0.0
    acc = 0.0
    for s, v in zip(scores_t, values):
        p = jnp.exp(s - mx)
        den = den + jnp.sum(p, axis=0, keepdims=True)
        acc = acc + _dot_tn(v, p.astype(BF16))
    return (acc / den).T


def _swa_group(q_ref, j, k, v, lo, extra_keys, extra_values, sink_ref, valid_t):
    m = q_ref.shape[1]
    grp = SWA_HEADS // SWA_KV_HEADS
    qs = []
    for pp in range(grp // 2):
        qp = q_ref[0, :, (j * grp // 2 + pp) * PAIR:(j * grp // 2 + pp + 1) * PAIR]
        qs += [_keep_head(qp, lo, 0), _keep_head(qp, lo, 1)]
    qg = jnp.concatenate(qs, axis=0)
    sink = jnp.concatenate([jnp.full((1, m), sink_ref[j * grp + g], F32) for g in range(grp)], axis=1)
    scores, values = [], []
    if k is not None:
        scores.append(jnp.where(valid_t, _dot_nt(k, qg), NEG_INF))
        values.append(v)
    for ke, ve in zip(extra_keys, extra_values):
        scores.append(_dot_nt(ke, qg))
        values.append(ve)
    o = _softmax_pv_t(scores, values, sink)
    return [jnp.where(lo, o[2 * pp * m:(2 * pp + 1) * m], o[(2 * pp + 1) * m:(2 * pp + 2) * m])
            for pp in range(grp // 2)]


def _swa_kernel(sink_ref, q_ref, k_ref, v_ref, kc_ref, vc_ref, o_ref):
    blk = SWA_BLOCK
    win = 3 * blk
    seq = k_ref.shape[1]
    grp = SWA_HEADS // SWA_KV_HEADS
    i = pl.program_id(1)
    start = pl.multiple_of(jnp.clip((i - 1) * blk, 0, seq - win), blk)
    lo = _first_head_lanes()
    kpos = start + lax.broadcasted_iota(jnp.int32, (win, blk), 0)
    qpos = i * blk + lax.broadcasted_iota(jnp.int32, (win, blk), 1)
    valid = jnp.abs(kpos - qpos) <= SWA_WINDOW
    valid = jnp.concatenate([valid] * grp, axis=1)
    for j in range(SWA_KV_HEADS):
        sl = slice(j * PAIR, (j + 1) * PAIR)
        outs = _swa_group(q_ref, j, k_ref[0, pl.ds(start, win), sl], v_ref[0, pl.ds(start, win), sl], lo,
                          [kc_ref[0, :, sl]], [vc_ref[0, :, sl]], sink_ref, valid)
        for pp, o in enumerate(outs):
            o_ref[0, :, (j * grp // 2 + pp) * PAIR:(j * grp // 2 + pp + 1) * PAIR] = o.astype(o_ref.dtype)


def _swa(sink, p, pc):
    bsz, seq, w = p["swa_q"].shape
    t = pc["swa_k"].shape[1]
    kvw = SWA_KV_HEADS * PAIR
    kv = pl.BlockSpec((1, seq, kvw), lambda b, i: (b, 0, 0))
    kvc = pl.BlockSpec((1, t, kvw), lambda b, i: (b, 0, 0))
    qb = pl.BlockSpec((1, SWA_BLOCK, w), lambda b, i: (b, i, 0))
    return pl.pallas_call(
        _swa_kernel,
        grid=(bsz, seq // SWA_BLOCK),
        in_specs=[pl.BlockSpec(memory_space=pltpu.SMEM), qb, kv, kv, kvc, kvc],
        out_specs=qb,
        out_shape=jax.ShapeDtypeStruct((bsz, seq, w), BF16),
        compiler_params=_params(("parallel", "parallel"), 32),
        name="swa",
    )(sink, p["swa_q"], p["swa_k"], p["swa_v"], pc["swa_k"], pc["swa_v"])


NA_GROUP = 4
NA_BAND = 12
NA_GRID_ROWS = 32
_NA_CFGS = ((0, 0), (4, 0), (8, 4), (28, 20))


def _na_bias_kernel(rpb_ref, o_ref):
    hp = pl.program_id(0)
    gw, wc = GRID_W, NA_COLS
    rows = NA_GRID_ROWS
    ck = lax.broadcasted_iota(jnp.int32, (gw, gw), 0)
    cq = lax.broadcasted_iota(jnp.int32, (gw, gw), 1)
    col_start = jnp.clip(cq - wc // 2, 0, gw - wc)
    col_ok = (ck >= col_start) & (ck < col_start + wc)
    dc = jnp.clip(ck - cq, -(wc - 1), wc - 1) + (wc - 1)
    neg = jnp.full((gw, gw), NEG_INF, F32)
    per_dr = []
    for j in range(2):
        tiles = []
        for dr in range(2 * NA_ROWS - 1):
            t = jnp.zeros((gw, gw), F32)
            for d in range(2 * wc - 1):
                t = jnp.where(dc == d, rpb_ref[2 * hp + j, dr, d], t)
            tiles.append(jnp.where(col_ok, t, NEG_INF))
        per_dr.append(tiles)
    for c, (r0, start) in enumerate(_NA_CFGS):
        for jp in range(NA_BAND):
            krow = start + jp
            blocks = []
            for j in range(2):
                for i in range(NA_GROUP):
                    r = r0 + i
                    band0 = min(max(r - NA_ROWS // 2, 0), rows - NA_ROWS)
                    blocks.append(per_dr[j][krow - r + NA_ROWS - 1] if band0 <= krow < band0 + NA_ROWS else neg)
            o_ref[c, 0, jp * gw:(jp + 1) * gw, :] = jnp.concatenate(blocks, axis=1)


def _na_bias(rpb):
    n_cfg = len(_NA_CFGS)
    blk = (n_cfg, 1, NA_BAND * GRID_W, 2 * NA_GROUP * GRID_W)
    return pl.pallas_call(
        _na_bias_kernel,
        grid=(NA_HEADS // 2,),
        in_specs=[pl.BlockSpec(memory_space=pltpu.SMEM)],
        out_specs=pl.BlockSpec(blk, lambda h: (0, h, 0, 0)),
        out_shape=jax.ShapeDtypeStruct((n_cfg, NA_HEADS // 2) + blk[2:], F32),
        compiler_params=_params(("parallel",), 32),
        name="na_bias",
    )(rpb)


def _na_pair(qp, lo, keys, values, bias):
    m = qp.shape[0]
    q2 = jnp.concatenate([_keep_head(qp, lo, 0), _keep_head(qp, lo, 1)], axis=0)
    scores = [_dot_nt(k, q2) for k in keys]
    if bias is not None:
        scores[0] = scores[0] + bias
    o = _softmax_pv_t(scores, values)
    return jnp.where(lo, o[:m], o[m:])


def _na_kernel(q_ref, k_ref, v_ref, kc_ref, vc_ref, bias_ref, o_ref):
    rows = k_ref.shape[1] // GRID_W
    nk = NA_BAND * GRID_W
    g = pl.program_id(1)
    first = jnp.clip(g * NA_GROUP - NA_ROWS // 2, 0, rows - NA_BAND)
    st = pl.multiple_of(first * GRID_W, GRID_W)
    lo = _first_head_lanes()
    for hp in range(NA_HEADS // 2):
        sl = slice(hp * PAIR, (hp + 1) * PAIR)
        o = _na_pair(q_ref[0, :, sl], lo, [k_ref[0, pl.ds(st, nk), sl], kc_ref[0, :, sl]],
                     [v_ref[0, pl.ds(st, nk), sl], vc_ref[0, :, sl]], bias_ref[0, hp])
        o_ref[0, :, sl] = o.astype(o_ref.dtype)


def _na(bias, p, pc):
    bsz, seq, w = p["na_q"].shape
    t = pc["na_k"].shape[1]
    rows = seq // GRID_W
    assert rows == NA_GRID_ROWS and bias.shape[0] == len(_NA_CFGS)
    n_groups = rows // NA_GROUP
    kv = pl.BlockSpec((1, seq, w), lambda b, g: (b, 0, 0))
    kvc = pl.BlockSpec((1, t, w), lambda b, g: (b, 0, 0))
    qb = pl.BlockSpec((1, NA_GROUP * GRID_W, w), lambda b, g: (b, g, 0))

    def cfg(b, g):
        return (jnp.where(g < 2, g, jnp.where(g < n_groups - 1, 2, 3)), 0, 0, 0)

    return pl.pallas_call(
        _na_kernel,
        grid=(bsz, n_groups),
        in_specs=[qb, kv, kv, kvc, kvc,
                  pl.BlockSpec((1, NA_HEADS // 2, NA_BAND * GRID_W, 2 * NA_GROUP * GRID_W), cfg)],
        out_specs=qb,
        out_shape=jax.ShapeDtypeStruct((bsz, seq, w), BF16),
        compiler_params=_params(("parallel", "arbitrary"), 48),
        name="na",
    )(p["na_q"], p["na_k"], p["na_v"], pc["na_k"], pc["na_v"], bias)


def _ctx_attn_kernel(sink_ref, sq_ref, sk_ref, sv_ref, nq_ref, nk_ref, nv_ref, so_ref, no_ref):
    grp = SWA_HEADS // SWA_KV_HEADS
    lo = _first_head_lanes()
    for j in range(SWA_KV_HEADS):
        sl = slice(j * PAIR, (j + 1) * PAIR)
        outs = _swa_group(sq_ref, j, None, None, lo, [sk_ref[0, :, sl]], [sv_ref[0, :, sl]], sink_ref, None)
        for pp, o in enumerate(outs):
            so_ref[0, :, (j * grp // 2 + pp) * PAIR:(j * grp // 2 + pp + 1) * PAIR] = o.astype(so_ref.dtype)
    for hp in range(NA_HEADS // 2):
        sl = slice(hp * PAIR, (hp + 1) * PAIR)
        o = _na_pair(nq_ref[0, :, sl], lo, [nk_ref[0, :, sl]], [nv_ref[0, :, sl]], None)
        no_ref[0, :, sl] = o.astype(no_ref.dtype)


def _ctx_attn(sink, pc):
    bsz, t, w = pc["swa_q"].shape
    kvw = SWA_KV_HEADS * PAIR
    full = pl.BlockSpec((1, t, w), lambda b: (b, 0, 0))
    kv = pl.BlockSpec((1, t, kvw), lambda b: (b, 0, 0))
    return pl.pallas_call(
        _ctx_attn_kernel,
        grid=(bsz,),
        in_specs=[pl.BlockSpec(memory_space=pltpu.SMEM), full, kv, kv, full, full, full],
        out_specs=[full, full],
        out_shape=[jax.ShapeDtypeStruct((bsz, t, w), BF16)] * 2,
        compiler_params=_params(("parallel",), 32),
        name="ctx_attn",
    )(sink, pc["swa_q"], pc["swa_k"], pc["swa_v"], pc["na_q"], pc["na_k"], pc["na_v"])


def _outproj_kernel(yr_ref, yc_ref, ys_ref, yn_ref, w_ref, x_ref, g1_ref, gain_ref, sc_ref, sh_ref, wr_ref,
                    xo_ref, h_ref, lg_ref):
    gw = GROUP_WIDTH
    acc = _dot(yr_ref[0], w_ref[0:gw, :])
    acc += _dot(yc_ref[0], w_ref[gw:2 * gw, :])
    acc += _dot(ys_ref[0], w_ref[2 * gw:3 * gw, :])
    acc += _dot(yn_ref[0], w_ref[3 * gw:4 * gw, :])
    xn = x_ref[0] + g1_ref[0] * acc
    xo_ref[0] = xn
    hb = _rms_modulate(xn, gain_ref[...], sc_ref[0], sh_ref[0]).astype(BF16)
    h_ref[0] = hb
    lg_ref[0] = _dot(hb, wr_ref[...])


def _outproj(ys, w_out, x, g1, gain, sc, sh, w_router, *, tm):
    bsz, seq, d = x.shape
    yb = pl.BlockSpec((1, tm, GROUP_WIDTH), lambda b, i: (b, i, 0))
    xb = pl.BlockSpec((1, tm, d), lambda b, i: (b, i, 0))
    vec = pl.BlockSpec((1, 1, d), lambda b, i: (b, 0, 0))
    return pl.pallas_call(
        _outproj_kernel,
        grid=(bsz, seq // tm),
        in_specs=[yb, yb, yb, yb,
                  pl.BlockSpec((d, d), lambda b, i: (0, 0), pipeline_mode=pl.Buffered(1)),
                  xb, vec, pl.BlockSpec((1, d), lambda b, i: (0, 0)), vec, vec,
                  pl.BlockSpec((d, N_EXPERTS), lambda b, i: (0, 0))],
        out_specs=[xb, xb, pl.BlockSpec((1, tm, N_EXPERTS), lambda b, i: (b, i, 0))],
        out_shape=[jax.ShapeDtypeStruct((bsz, seq, d), F32), jax.ShapeDtypeStruct((bsz, seq, d), BF16),
                   jax.ShapeDtypeStruct((bsz, seq, N_EXPERTS), F32)],
        compiler_params=_params(("parallel", "parallel"), 48),
        name="outproj",
    )(*ys, w_out, x, g1, gain.reshape(1, d), sc, sh, w_router.astype(BF16))


def _cumsum_lanes(x, tri):
    outs = []
    carry = jnp.zeros((x.shape[0], 1), F32)
    for c0 in range(0, x.shape[1], 128):
        cs = _dot(x[:, c0:c0 + 128], tri) + carry
        outs.append(cs)
        carry = cs[:, 127:128]
    return jnp.concatenate(outs, axis=1) if len(outs) > 1 else outs[0]


def _router_kernel(lg_ref, pos_ref, gate_ref, *, cap):
    lg = lg_ref[0]
    e = jnp.exp(lg - jnp.max(lg, axis=0, keepdims=True))
    aff = e / jnp.sum(e, axis=0, keepdims=True)
    capf = float(cap)
    lo = jnp.zeros((lg.shape[0], 1), jnp.int32)
    for bit in range(30, -1, -1):
        cand = lo | (1 << bit)
        cnt = jnp.sum(jnp.where(aff >= pltpu.bitcast(cand, F32), 1.0, 0.0), axis=1, keepdims=True)
        lo = jnp.where(cnt >= capf, cand, lo)
    kth = pltpu.bitcast(lo, F32)
    gt = aff > kth
    eq = aff == kth
    need = capf - jnp.sum(jnp.where(gt, 1.0, 0.0), axis=1, keepdims=True)
    ri = lax.broadcasted_iota(jnp.int32, (128, 128), 0)
    ci = lax.broadcasted_iota(jnp.int32, (128, 128), 1)
    tri = jnp.where(ri <= ci, 1.0, 0.0).astype(BF16)
    eq_rank = _cumsum_lanes(jnp.where(eq, 1.0, 0.0).astype(BF16), tri)
    sel = gt | (eq & (eq_rank <= need))
    slot = _cumsum_lanes(jnp.where(sel, 1.0, 0.0).astype(BF16), tri) - 1.0
    pos_ref[0] = jnp.where(sel, slot, -1.0).astype(jnp.int32)
    gate_ref[0] = aff


def _router(logits_t, cap):
    bsz, n_e, t = logits_t.shape
    blk = pl.BlockSpec((1, n_e, t), lambda b: (b, 0, 0))
    return pl.pallas_call(
        functools.partial(_router_kernel, cap=cap),
        grid=(bsz,),
        in_specs=[blk],
        out_specs=[blk, blk],
        out_shape=[jax.ShapeDtypeStruct((bsz, n_e, t), jnp.int32), jax.ShapeDtypeStruct((bsz, n_e, t), F32)],
        compiler_params=_params(("parallel",), 32),
        name="router",
    )(logits_t)


EXPERT_ROWS = 256


EXPERT_W_CHUNKS = 8


def _expert_kernel(pos_ref, gate_ref, h_ref, wg_ref, wu_ref, wd_ref, o_ref, wg_s, wu_s, wd_s, *, cap, n_e, n_compute):
    g, c = pl.program_id(0), pl.program_id(1)
    nb, t = h_ref.shape[0], h_ref.shape[1]

    @pl.when(g < n_e)
    def _():
        slot = g % 2
        rg, rd = wg_ref.shape[0], wd_ref.shape[0]
        wg_s[slot, pl.ds(pl.multiple_of(c * rg, rg), rg), :] = wg_ref[...].astype(BF16)
        wu_s[slot, pl.ds(pl.multiple_of(c * rg, rg), rg), :] = wu_ref[...].astype(BF16)
        wd_s[slot, pl.ds(pl.multiple_of(c * rd, rd), rd), :] = wd_ref[...].astype(BF16)

    @pl.when((g > 0) & (c < n_compute))
    def _():
        cur = (g - 1) % 2
        slot_id = lax.broadcasted_iota(jnp.int32, (cap, t), 0)
        rows, gates = [], []
        for i in range(nb):
            hit = slot_id == pos_ref[i, 0]
            rows.append(_dot(jnp.where(hit, 1.0, 0.0).astype(BF16), h_ref[i]).astype(BF16))
            gates.append(jnp.sum(jnp.where(hit, gate_ref[i, 0], 0.0), axis=1, keepdims=True))
        xs = jnp.concatenate(rows, axis=0) if nb > 1 else rows[0]
        gate = jnp.concatenate(gates, axis=0) if nb > 1 else gates[0]
        a = _dot(xs, wg_s[cur])
        u = _dot(xs, wu_s[cur])
        ye = (_dot((_silu(a) * u).astype(BF16), wd_s[cur]) * gate).astype(o_ref.dtype)
        for i in range(nb):
            o_ref[i, 0] = ye[i * cap:(i + 1) * cap]


def _experts(pos, gate, h, wg, wu, wd, layer, cap):
    bsz, t, d = h.shape
    _, n_e, _, ff = wg.shape
    nb = min(bsz, max(1, EXPERT_ROWS // cap))
    nc = EXPERT_W_CHUNKS
    n_compute = bsz // nb
    assert bsz % nb == 0 and n_compute <= nc and d % nc == 0 and ff % nc == 0

    def sample(g, c):
        return jnp.where(g == 0, 0, jnp.minimum(c, n_compute - 1))

    def act(g, c):
        return (sample(g, c), jnp.maximum(g - 1, 0), 0, 0)

    def wchunk(g, c):
        return (layer, jnp.minimum(g, n_e - 1), jnp.where(g < n_e, c, nc - 1), 0)

    row = pl.BlockSpec((nb, 1, 1, t), act)
    return pl.pallas_call(
        functools.partial(_expert_kernel, cap=cap, n_e=n_e, n_compute=n_compute),
        grid=(n_e + 1, nc),
        in_specs=[row, row, pl.BlockSpec((nb, t, d), lambda g, c: (sample(g, c), 0, 0)),
                  pl.BlockSpec((None, None, d // nc, ff), wchunk),
                  pl.BlockSpec((None, None, d // nc, ff), wchunk),
                  pl.BlockSpec((None, None, ff // nc, d), wchunk)],
        out_specs=pl.BlockSpec((nb, 1, cap, d), act),
        out_shape=jax.ShapeDtypeStruct((bsz, n_e, cap, d), BF16),
        scratch_shapes=[pltpu.VMEM((2, d, ff), BF16), pltpu.VMEM((2, d, ff), BF16), pltpu.VMEM((2, ff, d), BF16)],
        compiler_params=_params(("arbitrary", "arbitrary"), 56),
        name="experts",
    )(pos.reshape(bsz, n_e, 1, t), gate.reshape(bsz, n_e, 1, t), h, wg, wu, wd)


def _combine_kernel(pos_ref, ye_ref, x_ref, g2_ref, *rest, cap):
    o_ref = rest[-1]
    n_e = pos_ref.shape[1]
    tb = x_ref.shape[1]
    slot = lax.broadcasted_iota(jnp.int32, (cap, tb), 0)
    hit = jnp.concatenate([jnp.where(slot == pos_ref[0, e:e + 1, :], 1.0, 0.0).astype(BF16) for e in range(n_e)],
                          axis=0)
    xn = x_ref[0] + g2_ref[0] * _dot_tn(hit, ye_ref[0])
    if len(rest) == 2:
        ms = jnp.mean(xn * xn, axis=-1, keepdims=True)
        xn = (xn * lax.rsqrt(ms + EPS)) * rest[0][...]
    o_ref[0] = xn


def _combine(pos, ye, x, g2, cap, tb, final_gain=None):
    bsz, t, d = x.shape
    n_e = pos.shape[1]
    xb = pl.BlockSpec((1, tb, d), lambda b, i: (b, i, 0))
    in_specs = [pl.BlockSpec((1, n_e, tb), lambda b, i: (b, 0, i)),
                pl.BlockSpec((1, n_e * cap, d), lambda b, i: (b, 0, 0)),
                xb, pl.BlockSpec((1, 1, d), lambda b, i: (b, 0, 0))]
    args = [pos, ye.reshape(bsz, n_e * cap, d), x, g2]
    if final_gain is not None:
        in_specs.append(pl.BlockSpec((1, d), lambda b, i: (0, 0)))
        args.append(final_gain.reshape(1, d))
    return pl.pallas_call(
        functools.partial(_combine_kernel, cap=cap),
        grid=(bsz, t // tb),
        in_specs=in_specs,
        out_specs=xb,
        out_shape=jax.ShapeDtypeStruct((bsz, t, d), F32),
        compiler_params=_params(("parallel", "arbitrary"), 56),
        name="combine",
    )(*args)


def _moe(x, h, logits, g2, lw, final_gain=None):
    bsz, t, d = x.shape
    cap = EC_CAPACITY_FACTOR * t // N_EXPERTS
    pos, gate = _router(jnp.swapaxes(logits, 1, 2), cap)
    ye = _experts(pos, gate, h, lw["w_gate"], lw["w_up"], lw["w_down"], lw["layer"], cap)
    return _combine(pos, ye, x, g2, cap, min(t, 512), final_gain)


def _layer(x, xc, mods, mods_c, lw, need_ctx, final_gain):
    sh1, sc1, g1, sh2, sc2, g2 = mods
    sh1c, sc1c, g1c, sh2c, sc2c, g2c = mods_c
    names_lat = tuple(p[0] for p in _PIECES)
    p = _inproj(x, lw["norm_mix"], sc1, sh1, lw["w_in"], rope=True, names=names_lat, tm=256)
    pc = _inproj(xc, lw["norm_mix"], sc1c, sh1c, lw["w_in"], rope=False,
                 names=names_lat if need_ctx else _CTX_KV_ONLY, tm=256)

    y_ret, yc_ret = _retention(lw["ret_decay"], p, pc, need_ctx)
    y_conv = _short_conv(lw["conv_w"], p["conv_b"], p["conv_c"], p["conv_h"])
    y_swa = _swa(lw["swa_sink"], p, pc)
    y_na = _na(_na_bias(lw["na_rpb"]), p, pc)
    x, h, logits = _outproj((y_ret, y_conv, y_swa, y_na), lw["w_out"], x, g1, lw["norm_ffn"], sc2, sh2,
                            lw["w_router"], tm=256)
    x = _moe(x, h, logits, g2, lw, final_gain)
    if need_ctx:
        yc_conv = _short_conv(lw["conv_w"], pc["conv_b"], pc["conv_c"], pc["conv_h"])
        yc_swa, yc_na = _ctx_attn(lw["swa_sink"], pc)
        xc, hc, logits_c = _outproj((yc_ret, yc_conv, yc_swa, yc_na), lw["w_out"], xc, g1c, lw["norm_ffn"],
                                    sc2c, sh2c, lw["w_router"], tm=256)
        xc = _moe(xc, hc, logits_c, g2c, lw)
    return x, xc


def kernel(x, c, ctx, c_ctx, w_ada, b_ada, norm_mix, norm_ffn, w_in, w_out, ret_decay_fwd, ret_decay_bwd,
           conv_w, swa_sink, na_rpb, w_router, w_gate, w_up, w_down, norm_final):
    bsz, _, d = x.shape
    depth = w_ada.shape[0]
    assert bsz + 1 <= ADA_ROWS
    cs = jnp.zeros((ADA_ROWS, d), F32).at[:bsz].set(c).at[bsz].set(c_ctx)
    ada = _ada(cs, w_ada, b_ada)
    w_out_b = w_out.astype(BF16)
    xc = ctx
    for l in range(depth):
        mods = tuple(ada[l, :bsz, i * d:(i + 1) * d].reshape(bsz, 1, d) for i in range(6))
        mods_c = tuple(jnp.broadcast_to(ada[l, bsz, i * d:(i + 1) * d].reshape(1, 1, d), (bsz, 1, d))
                       for i in range(6))
        lw = dict(norm_mix=norm_mix[l], norm_ffn=norm_ffn[l], w_in=_widen_swa_kv(w_in[l].astype(BF16)),
                  w_out=w_out_b[l], ret_decay=jnp.stack([ret_decay_fwd[l], ret_decay_bwd[l]]), conv_w=conv_w[l],
                  swa_sink=swa_sink[l], na_rpb=na_rpb[l], w_router=w_router[l],
                  w_gate=w_gate, w_up=w_up, w_down=w_down, layer=l)
        last = l == depth - 1
        x, xc = _layer(x, xc, mods, mods_c, lw, need_ctx=not last, final_gain=norm_final if last else None)
    return x
```

```python
import functools

import numpy as np
import jax
import jax.numpy as jnp
from jax import lax
from jax.experimental import pallas as pl
from jax.experimental.pallas import tpu as pltpu

D_MODEL = 2048
DEPTH = 2
GRID_W = 64
HEAD_DIM = 64
GROUP_WIDTH = D_MODEL // 4
RET_CHUNK = 128
SWA_HEADS = 8
SWA_KV_HEADS = 2
SWA_WINDOW = 128
SWA_BLOCK = 128
NA_HEADS = 8
NA_ROWS = 8
NA_COLS = 16
N_EXPERTS = 16
EXPERT_FF = D_MODEL // 2
EC_CAPACITY_FACTOR = 2
ROPE_BASE = 10000.0
EPS = 1e-6
NEG_INF = -1e30
F32 = jnp.float32
BF16 = jnp.bfloat16
ADA_ROWS = 16
QK_SCALE = HEAD_DIM ** -0.5
LOG2E = 1.4426950408889634
PAIR = 2 * HEAD_DIM
TOKEN_TILE = 512
CTX_TOKEN_TILE = 256

_PIECES = (
    ("ret_q", 0, 512, True, 1.0, False),
    ("ret_k", 512, 512, True, QK_SCALE, False),
    ("ret_v", 1024, 512, False, 1.0, False),
    ("ret_g", 1536, 512, False, 1.0, False),
    ("conv_b", 2048, 512, False, 1.0, False),
    ("conv_c", 2560, 512, False, 1.0, False),
    ("conv_h", 3072, 512, False, 1.0, False),
    ("swa_q", 3584, 512, True, QK_SCALE * LOG2E, False),
    ("swa_k", 4096, 128, True, 1.0, True),
    ("swa_v", 4224, 128, False, 1.0, True),
    ("na_q", 4352, 512, False, QK_SCALE * LOG2E, False),
    ("na_k", 4864, 512, False, 1.0, False),
    ("na_v", 5376, 512, False, 1.0, False),
)
IN_COLS = 5888
_CTX_KV_ONLY = ("ret_k", "ret_v", "swa_k", "swa_v", "na_k", "na_v")


def _params(sem, vmem_mb):
    return pltpu.CompilerParams(dimension_semantics=sem, vmem_limit_bytes=vmem_mb << 20)


def _silu(x):
    return x * jax.nn.sigmoid(x)


def _dot(a, b):
    return jnp.dot(a, b, preferred_element_type=F32)


def _dot_nt(a, b):
    return lax.dot_general(a, b, (((1,), (1,)), ((), ())), preferred_element_type=F32)


def _dot_tn(a, b):
    return lax.dot_general(a, b, (((0,), (0,)), ((), ())), preferred_element_type=F32)


def _first_head_lanes():
    return lax.broadcasted_iota(jnp.int32, (1, PAIR), 1) < HEAD_DIM


def _keep_head(x, lo, j):
    return jnp.where(lo if j == 0 else jnp.logical_not(lo), x, jnp.zeros_like(x))


def _ada_kernel(c_ref, w_ref, b_ref, o_ref):
    s = _silu(c_ref[...]).astype(BF16)
    o_ref[0] = _dot(s, w_ref[0].astype(BF16)) + b_ref[0]


def _ada(cs, w_ada, b_ada):
    depth, d, n = w_ada.shape
    tn = 1024
    return pl.pallas_call(
        _ada_kernel,
        grid=(depth, n // tn),
        in_specs=[
            pl.BlockSpec((ADA_ROWS, d), lambda l, j: (0, 0)),
            pl.BlockSpec((1, d, tn), lambda l, j: (l, 0, j)),
            pl.BlockSpec((1, 1, tn), lambda l, j: (l, 0, j)),
        ],
        out_specs=pl.BlockSpec((1, ADA_ROWS, tn), lambda l, j: (l, 0, j)),
        out_shape=jax.ShapeDtypeStruct((depth, ADA_ROWS, n), F32),
        compiler_params=_params(("arbitrary", "arbitrary"), 40),
        name="ada",
    )(cs, w_ada, b_ada.reshape(depth, 1, n))


def _rope_tables(seq):
    half = HEAD_DIM // 2
    nf = half // 2
    t = np.arange(seq)
    row = (t // GRID_W).astype(np.float32)
    col = (t % GRID_W).astype(np.float32)
    inv = (np.float32(ROPE_BASE) ** (-np.arange(nf, dtype=np.float32) / np.float32(nf))).astype(np.float32)
    ang = np.concatenate([row[:, None] * inv, col[:, None] * inv], axis=-1).astype(np.float32)
    cos, sin = np.cos(ang), np.sin(ang)
    cos64 = np.concatenate([cos, cos], axis=-1)
    sin64 = np.concatenate([-sin, sin], axis=-1)
    return (np.tile(cos64, (1, 2)).astype(np.float32), np.tile(sin64, (1, 2)).astype(np.float32))


def _rope(z, cos, sin_signed):
    w = z.shape[1]
    reps = w // 128
    cos_w = jnp.concatenate([cos] * reps, axis=1) if reps > 1 else cos
    sin_w = jnp.concatenate([sin_signed] * reps, axis=1) if reps > 1 else sin_signed
    lane = lax.broadcasted_iota(jnp.int32, (1, w), 1)
    first_half = (lane & (HEAD_DIM - 1)) < (HEAD_DIM // 2)
    up = pltpu.roll(z, w - HEAD_DIM // 2, 1)
    dn = pltpu.roll(z, HEAD_DIM // 2, 1)
    return z * cos_w + jnp.where(first_half, up, dn) * sin_w


def _rms_modulate(x, gain, scale, shift):
    ms = jnp.mean(x * x, axis=-1, keepdims=True)
    return (x * lax.rsqrt(ms + EPS)) * (gain * (1.0 + scale)) + shift


def _inproj_kernel(*refs, pieces, rope):
    if rope:
        x_ref, gain_ref, sc_ref, sh_ref, cos_ref, sin_ref, w_ref = refs[:7]
        o_refs = refs[7:]
    else:
        x_ref, gain_ref, sc_ref, sh_ref, w_ref = refs[:5]
        o_refs = refs[5:]
    hb = _rms_modulate(x_ref[0], gain_ref[...], sc_ref[0], sh_ref[0]).astype(BF16)
    for (_, off, width, do_rope, scale, dup), o_ref in zip(pieces, o_refs):
        z = _dot(hb, w_ref[:, off:off + width])
        if rope and do_rope:
            z = _rope(z, cos_ref[...], sin_ref[...])
        if scale != 1.0:
            z = z * scale
        if dup:
            heads = [z[:, j * HEAD_DIM:(j + 1) * HEAD_DIM] for j in range(width // HEAD_DIM)]
            z = jnp.concatenate([h for h in heads for _ in range(2)], axis=1)
        o_ref[0] = z.astype(BF16)


def _inproj(x, gain, sc, sh, w_in, *, rope, names, tm):
    bsz, seq, d = x.shape
    pieces = tuple(p for p in _PIECES if p[0] in names)
    widths = [p[2] * (2 if p[5] else 1) for p in pieces]
    vec = pl.BlockSpec((1, 1, d), lambda b, i: (b, 0, 0))
    in_specs = [pl.BlockSpec((1, tm, d), lambda b, i: (b, i, 0)),
                pl.BlockSpec((1, d), lambda b, i: (0, 0)), vec, vec]
    args = [x, gain.reshape(1, d), sc, sh]
    if rope:
        cos, sin = _rope_tables(seq)
        in_specs += [pl.BlockSpec((tm, 128), lambda b, i: (i, 0))] * 2
        args += [jnp.asarray(cos), jnp.asarray(sin)]
    in_specs.append(pl.BlockSpec((d, IN_COLS), lambda b, i: (0, 0), pipeline_mode=pl.Buffered(1)))
    args.append(w_in)
    outs = pl.pallas_call(
        functools.partial(_inproj_kernel, pieces=pieces, rope=rope),
        grid=(bsz, seq // tm),
        in_specs=in_specs,
        out_specs=[pl.BlockSpec((1, tm, w), lambda b, i: (b, i, 0)) for w in widths],
        out_shape=[jax.ShapeDtypeStruct((bsz, seq, w), BF16) for w in widths],
        compiler_params=_params(("parallel", "parallel"), 56),
        name="inproj_lat" if rope else "inproj_ctx",
    )(*args)
    return {p[0]: o for p, o in zip(pieces, outs)}


def _log_sigmoid(x):
    return jnp.minimum(x, 0.0) - jnp.log(1.0 + jnp.exp(-jnp.abs(x)))


RET_PAIRS = 2


def _ret_kernel(dec_ref, q_ref, k_ref, v_ref, g_ref, qc_ref, kc_ref, vc_ref, gc_ref, *rest, need_ctx):
    if need_ctx:
        y_ref, yc_ref, sb_ref = rest
    else:
        (y_ref, sb_ref), yc_ref = rest, None
    c = RET_CHUNK
    hd = HEAD_DIM
    n_lat = q_ref.shape[1] // c
    n_ctx = qc_ref.shape[1] // c
    n_pairs = q_ref.shape[2] // PAIR
    first_head = 2 * n_pairs * pl.program_id(1)

    lo = _first_head_lanes()
    ri = lax.broadcasted_iota(jnp.int32, (c, c), 0)
    ci = lax.broadcasted_iota(jnp.int32, (c, c), 1)
    diff = (ri - ci).astype(F32)
    same_head = (ri < hd) == (ci < hd)
    pos = lax.broadcasted_iota(jnp.int32, (c, PAIR), 0).astype(F32)

    consts = []
    for p in range(n_pairs):
        lg_f = [_log_sigmoid(jnp.full((1, c), dec_ref[0, first_head + 2 * p + j], F32)) for j in range(2)]
        lg_b = [_log_sigmoid(jnp.full((1, c), dec_ref[1, first_head + 2 * p + j], F32)) for j in range(2)]
        dmat = jnp.concatenate(
            [jnp.where(diff >= 0.0, jnp.exp(lg_f[j] * jnp.maximum(diff, 0.0)),
                       jnp.exp(lg_b[j] * jnp.maximum(-diff, 0.0))) for j in range(2)], axis=0)
        lf = jnp.where(lo, lg_f[0], lg_f[1])
        lb = jnp.where(lo, lg_b[0], lg_b[1])
        consts.append(dict(dmat=dmat, xi_f=jnp.exp(lf * (pos + 1.0)), xi_b=jnp.exp(lb * (c - pos)),
                           ze_f=jnp.exp(lf * (c - 1.0 - pos)), ze_b=jnp.exp(lb * pos),
                           cd_f=jnp.exp(lf * float(c)), cd_b=jnp.exp(lb * float(c))))

    def rows(ref, r0, p):
        return ref[0, pl.ds(r0, c), p * PAIR:(p + 1) * PAIR]

    def kv_outer(k, v, zeta):
        kz = (k.astype(F32) * zeta).astype(BF16)
        return jnp.where(same_head, _dot_tn(kz, v), 0.0)

    def half_sum(x):
        s_lo = jnp.sum(jnp.where(lo, x, 0.0), axis=-1, keepdims=True)
        s_all = jnp.sum(x, axis=-1, keepdims=True)
        return jnp.where(lo, s_lo, s_all - s_lo)

    def bwd_chunk(k_r, v_r, r0, slot, states):
        new = []
        for p in range(n_pairs):
            sb_ref[slot, p] = states[p]
            new.append(states[p] * consts[p]["cd_b"] + kv_outer(rows(k_r, r0, p), rows(v_r, r0, p), consts[p]["ze_b"]))
        return tuple(new)

    def fwd_chunk(q_r, k_r, v_r, g_r, o_r, r0, slot, states):
        new = []
        for p in range(n_pairs):
            cp = consts[p]
            k, v = rows(k_r, r0, p), rows(v_r, r0, p)
            if o_r is not None:
                q = rows(q_r, r0, p)
                q2 = jnp.concatenate([_keep_head(q, lo, 0), _keep_head(q, lo, 1)], axis=0)
                a = (_dot_nt(q2, k) * cp["dmat"]).astype(BF16)
                qf = q.astype(F32)
                lhs = jnp.concatenate([a[:c], a[c:], (qf * cp["xi_f"]).astype(BF16), (qf * cp["xi_b"]).astype(BF16)],
                                      axis=1)
                rhs = jnp.concatenate([_keep_head(v, lo, 0), _keep_head(v, lo, 1), states[p].astype(BF16),
                                       sb_ref[slot, p].astype(BF16)], axis=0)
                y = _dot(lhs, rhs)
                yc = y - half_sum(y) * (1.0 / hd)
                var = half_sum(yc * yc) * (1.0 / hd)
                out = _silu(rows(g_r, r0, p).astype(F32)) * (yc * lax.rsqrt(var + EPS))
                o_r[0, pl.ds(r0, c), p * PAIR:(p + 1) * PAIR] = out.astype(o_r.dtype)
            new.append(states[p] * cp["cd_f"] + kv_outer(k, v, cp["ze_f"]))
        return tuple(new)

    zero = tuple(jnp.zeros((PAIR, PAIR), F32) for _ in range(n_pairs))
    st = zero
    for cc in range(n_ctx - 1, -1, -1):
        st = bwd_chunk(kc_ref, vc_ref, cc * c, cc, st)

    def bwd_body(i, s):
        cl = n_lat - 1 - i
        return bwd_chunk(k_ref, v_ref, pl.multiple_of(cl * c, c), n_ctx + cl, s)

    lax.fori_loop(0, n_lat, bwd_body, st, unroll=4)

    st = zero
    for cc in range(n_ctx):
        st = fwd_chunk(qc_ref, kc_ref, vc_ref, gc_ref, yc_ref, cc * c, cc, st)

    def fwd_body(cl, s):
        return fwd_chunk(q_ref, k_ref, v_ref, g_ref, y_ref, pl.multiple_of(cl * c, c), n_ctx + cl, s)

    lax.fori_loop(0, n_lat, fwd_body, st, unroll=4)


def _retention(dec, p, pc, need_ctx):
    bsz, seq, w = p["ret_q"].shape
    t = pc["ret_k"].shape[1]
    bw = RET_PAIRS * PAIR
    lat = pl.BlockSpec((1, seq, bw), lambda b, h: (b, 0, h))
    ctx = pl.BlockSpec((1, t, bw), lambda b, h: (b, 0, h))
    out_specs, out_shape = [lat], [jax.ShapeDtypeStruct((bsz, seq, w), BF16)]
    if need_ctx:
        out_specs.append(ctx)
        out_shape.append(jax.ShapeDtypeStruct((bsz, t, w), BF16))
        qc, gc = pc["ret_q"], pc["ret_g"]
    else:
        qc, gc = pc["ret_k"], pc["ret_k"]
    n_chunks = (seq + t) // RET_CHUNK
    outs = pl.pallas_call(
        functools.partial(_ret_kernel, need_ctx=need_ctx),
        grid=(bsz, w // bw),
        in_specs=[pl.BlockSpec(memory_space=pltpu.SMEM), lat, lat, lat, lat, ctx, ctx, ctx, ctx],
        out_specs=out_specs,
        out_shape=out_shape,
        scratch_shapes=[pltpu.VMEM((n_chunks, RET_PAIRS, PAIR, PAIR), F32)],
        compiler_params=_params(("parallel", "parallel"), 32),
        name="retention",
    )(dec, p["ret_q"], p["ret_k"], p["ret_v"], p["ret_g"], qc, pc["ret_k"], pc["ret_v"], gc)
    return (outs[0], outs[1]) if need_ctx else (outs[0], None)


def _conv_kernel(w_ref, b_ref, c_ref, h_ref, o_ref):
    seq = b_ref.shape[1]
    rows = min(256, seq)
    halo = 16
    w0, w1, w2 = w_ref[0:1, :], w_ref[1:2, :], w_ref[2:3, :]
    for r0 in range(0, seq, rows):
        lo, hi = max(0, r0 - halo), min(seq, r0 + rows + halo)
        u = c_ref[0, lo:hi, :].astype(F32) * h_ref[0, lo:hi, :].astype(F32)
        n = hi - lo
        t = lo + lax.broadcasted_iota(jnp.int32, (n, 1), 0)
        prev = jnp.where(t == 0, 0.0, pltpu.roll(u, 1, 0))
        nxt = jnp.where(t == seq - 1, 0.0, pltpu.roll(u, n - 1, 0))
        y = prev * w0 + u * w1 + nxt * w2
        y = y[r0 - lo:r0 - lo + rows]
        o_ref[0, r0:r0 + rows, :] = (b_ref[0, r0:r0 + rows, :].astype(F32) * y).astype(o_ref.dtype)


def _short_conv(conv_w, b, c, h):
    bsz, seq, w = b.shape
    blk = pl.BlockSpec((1, seq, w), lambda i: (i, 0, 0))
    return pl.pallas_call(
        _conv_kernel,
        grid=(bsz,),
        in_specs=[pl.BlockSpec((3, w), lambda i: (0, 0)), blk, blk, blk],
        out_specs=blk,
        out_shape=jax.ShapeDtypeStruct((bsz, seq, w), BF16),
        compiler_params=_params(("parallel",), 48),
        name="short_conv",
    )(conv_w, b, c, h)


def _softmax_pv_t(scores_t, values, extra=None):
    mx = functools.reduce(jnp.maximum, [jnp.max(s, axis=0, keepdims=True) for s in scores_t])
    if extra is not None:
        mx = jnp.maximum(mx, extra)
    den = jnp.exp2(extra - mx) if extra is not None else 0.0
    acc = 0.0
    for s, v in zip(scores_t, values):
        p = jnp.exp2(s - mx)
        den = den + jnp.sum(p, axis=0, keepdims=True)
        acc = acc + _dot_tn(v, p.astype(BF16))
    return (acc / den).T


def _swa_group(q_ref, j, k, v, lo, extra_keys, extra_values, sink_ref, valid_t):
    m = q_ref.shape[1]
    grp = SWA_HEADS // SWA_KV_HEADS
    qs = []
    for pp in range(grp // 2):
        qp = q_ref[0, :, (j * grp // 2 + pp) * PAIR:(j * grp // 2 + pp + 1) * PAIR]
        qs += [_keep_head(qp, lo, 0), _keep_head(qp, lo, 1)]
    qg = jnp.concatenate(qs, axis=0)
    sink = jnp.concatenate([jnp.full((1, m), sink_ref[j * grp + g] * LOG2E, F32) for g in range(grp)], axis=1)
    scores, values = [], []
    if k is not None:
        scores.append(jnp.where(valid_t, _dot_nt(k, qg), NEG_INF))
        values.append(v)
    for ke, ve in zip(extra_keys, extra_values):
        scores.append(_dot_nt(ke, qg))
        values.append(ve)
    o = _softmax_pv_t(scores, values, sink)
    return [jnp.where(lo, o[2 * pp * m:(2 * pp + 1) * m], o[(2 * pp + 1) * m:(2 * pp + 2) * m])
            for pp in range(grp // 2)]


def _swa_kernel(sink_ref, q_ref, k_ref, v_ref, kc_ref, vc_ref, o_ref):
    blk = SWA_BLOCK
    win = 3 * blk
    seq = k_ref.shape[1]
    grp = SWA_HEADS // SWA_KV_HEADS
    i = pl.program_id(1)
    start = pl.multiple_of(jnp.clip((i - 1) * blk, 0, seq - win), blk)
    lo = _first_head_lanes()
    kpos = start + lax.broadcasted_iota(jnp.int32, (win, blk), 0)
    qpos = i * blk + lax.broadcasted_iota(jnp.int32, (win, blk), 1)
    valid = jnp.abs(kpos - qpos) <= SWA_WINDOW
    valid = jnp.concatenate([valid] * grp, axis=1)
    for j in range(SWA_KV_HEADS):
        sl = slice(j * PAIR, (j + 1) * PAIR)
        outs = _swa_group(q_ref, j, k_ref[0, pl.ds(start, win), sl], v_ref[0, pl.ds(start, win), sl], lo,
                          [kc_ref[0, :, sl]], [vc_ref[0, :, sl]], sink_ref, valid)
        for pp, o in enumerate(outs):
            o_ref[0, :, (j * grp // 2 + pp) * PAIR:(j * grp // 2 + pp + 1) * PAIR] = o.astype(o_ref.dtype)


def _swa(sink, p, pc):
    bsz, seq, w = p["swa_q"].shape
    t = pc["swa_k"].shape[1]
    kvw = SWA_KV_HEADS * PAIR
    kv = pl.BlockSpec((1, seq, kvw), lambda b, i: (b, 0, 0))
    kvc = pl.BlockSpec((1, t, kvw), lambda b, i: (b, 0, 0))
    qb = pl.BlockSpec((1, SWA_BLOCK, w), lambda b, i: (b, i, 0))
    return pl.pallas_call(
        _swa_kernel,
        grid=(bsz, seq // SWA_BLOCK),
        in_specs=[pl.BlockSpec(memory_space=pltpu.SMEM), qb, kv, kv, kvc, kvc],
        out_specs=qb,
        out_shape=jax.ShapeDtypeStruct((bsz, seq, w), BF16),
        compiler_params=_params(("parallel", "parallel"), 32),
        name="swa",
    )(sink, p["swa_q"], p["swa_k"], p["swa_v"], pc["swa_k"], pc["swa_v"])


NA_GROUP = 4
NA_BAND = NA_ROWS + NA_GROUP - 1
NA_GRID_ROWS = 32
_NA_CFGS = ((0, 0), (4, 0), (8, 4), (28, 21))


def _na_bias_kernel(rpb_ref, o_ref):
    hp = pl.program_id(0)
    gw, wc = GRID_W, NA_COLS
    rows = NA_GRID_ROWS
    ck = lax.broadcasted_iota(jnp.int32, (gw, gw), 0)
    cq = lax.broadcasted_iota(jnp.int32, (gw, gw), 1)
    col_start = jnp.clip(cq - wc // 2, 0, gw - wc)
    col_ok = (ck >= col_start) & (ck < col_start + wc)
    dc = jnp.clip(ck - cq, -(wc - 1), wc - 1) + (wc - 1)
    neg = jnp.full((gw, gw), NEG_INF, F32)
    per_dr = []
    for j in range(2):
        tiles = []
        for dr in range(2 * NA_ROWS - 1):
            t = jnp.zeros((gw, gw), F32)
            for d in range(2 * wc - 1):
                t = jnp.where(dc == d, rpb_ref[2 * hp + j, dr, d] * LOG2E, t)
            tiles.append(jnp.where(col_ok, t, NEG_INF))
        per_dr.append(tiles)
    for c, (r0, start) in enumerate(_NA_CFGS):
        for jp in range(NA_BAND):
            krow = start + jp
            blocks = []
            for j in range(2):
                for i in range(NA_GROUP):
                    r = r0 + i
                    band0 = min(max(r - NA_ROWS // 2, 0), rows - NA_ROWS)
                    blocks.append(per_dr[j][krow - r + NA_ROWS - 1] if band0 <= krow < band0 + NA_ROWS else neg)
            o_ref[c, 0, jp * gw:(jp + 1) * gw, :] = jnp.concatenate(blocks, axis=1)


def _na_bias(rpb):
    n_cfg = len(_NA_CFGS)
    blk = (n_cfg, 1, NA_BAND * GRID_W, 2 * NA_GROUP * GRID_W)
    return pl.pallas_call(
        _na_bias_kernel,
        grid=(NA_HEADS // 2,),
        in_specs=[pl.BlockSpec(memory_space=pltpu.SMEM)],
        out_specs=pl.BlockSpec(blk, lambda h: (0, h, 0, 0)),
        out_shape=jax.ShapeDtypeStruct((n_cfg, NA_HEADS // 2) + blk[2:], F32),
        compiler_params=_params(("parallel",), 32),
        name="na_bias",
    )(rpb)


def _na_pair(qp, lo, keys, values, bias):
    m = qp.shape[0]
    q2 = jnp.concatenate([_keep_head(qp, lo, 0), _keep_head(qp, lo, 1)], axis=0)
    scores = [_dot_nt(k, q2) for k in keys]
    if bias is not None:
        scores[0] = scores[0] + bias
    o = _softmax_pv_t(scores, values)
    return jnp.where(lo, o[:m], o[m:])


def _na_kernel(q_ref, k_ref, v_ref, kc_ref, vc_ref, bias_ref, o_ref):
    rows = k_ref.shape[1] // GRID_W
    nk = NA_BAND * GRID_W
    g = pl.program_id(1)
    first = jnp.clip(g * NA_GROUP - NA_ROWS // 2, 0, rows - NA_BAND)
    st = pl.multiple_of(first * GRID_W, GRID_W)
    lo = _first_head_lanes()
    for hp in range(NA_HEADS // 2):
        sl = slice(hp * PAIR, (hp + 1) * PAIR)
        o = _na_pair(q_ref[0, :, sl], lo, [k_ref[0, pl.ds(st, nk), sl], kc_ref[0, :, sl]],
                     [v_ref[0, pl.ds(st, nk), sl], vc_ref[0, :, sl]], bias_ref[0, hp])
        o_ref[0, :, sl] = o.astype(o_ref.dtype)


def _na(bias, p, pc):
    bsz, seq, w = p["na_q"].shape
    t = pc["na_k"].shape[1]
    rows = seq // GRID_W
    assert rows == NA_GRID_ROWS and bias.shape[0] == len(_NA_CFGS)
    n_groups = rows // NA_GROUP
    kv = pl.BlockSpec((1, seq, w), lambda b, g: (b, 0, 0))
    kvc = pl.BlockSpec((1, t, w), lambda b, g: (b, 0, 0))
    qb = pl.BlockSpec((1, NA_GROUP * GRID_W, w), lambda b, g: (b, g, 0))

    def cfg(b, g):
        return (jnp.where(g < 2, g, jnp.where(g < n_groups - 1, 2, 3)), 0, 0, 0)

    return pl.pallas_call(
        _na_kernel,
        grid=(bsz, n_groups),
        in_specs=[qb, kv, kv, kvc, kvc,
                  pl.BlockSpec((1, NA_HEADS // 2, NA_BAND * GRID_W, 2 * NA_GROUP * GRID_W), cfg)],
        out_specs=qb,
        out_shape=jax.ShapeDtypeStruct((bsz, seq, w), BF16),
        compiler_params=_params(("parallel", "arbitrary"), 48),
        name="na",
    )(p["na_q"], p["na_k"], p["na_v"], pc["na_k"], pc["na_v"], bias)


def _ctx_attn_kernel(sink_ref, sq_ref, sk_ref, sv_ref, nq_ref, nk_ref, nv_ref, so_ref, no_ref):
    grp = SWA_HEADS // SWA_KV_HEADS
    lo = _first_head_lanes()
    for j in range(SWA_KV_HEADS):
        sl = slice(j * PAIR, (j + 1) * PAIR)
        outs = _swa_group(sq_ref, j, None, None, lo, [sk_ref[0, :, sl]], [sv_ref[0, :, sl]], sink_ref, None)
        for pp, o in enumerate(outs):
            so_ref[0, :, (j * grp // 2 + pp) * PAIR:(j * grp // 2 + pp + 1) * PAIR] = o.astype(so_ref.dtype)
    for hp in range(NA_HEADS // 2):
        sl = slice(hp * PAIR, (hp + 1) * PAIR)
        o = _na_pair(nq_ref[0, :, sl], lo, [nk_ref[0, :, sl]], [nv_ref[0, :, sl]], None)
        no_ref[0, :, sl] = o.astype(no_ref.dtype)


def _ctx_attn(sink, pc):
    bsz, t, w = pc["swa_q"].shape
    kvw = SWA_KV_HEADS * PAIR
    full = pl.BlockSpec((1, t, w), lambda b: (b, 0, 0))
    kv = pl.BlockSpec((1, t, kvw), lambda b: (b, 0, 0))
    return pl.pallas_call(
        _ctx_attn_kernel,
        grid=(bsz,),
        in_specs=[pl.BlockSpec(memory_space=pltpu.SMEM), full, kv, kv, full, full, full],
        out_specs=[full, full],
        out_shape=[jax.ShapeDtypeStruct((bsz, t, w), BF16)] * 2,
        compiler_params=_params(("parallel",), 32),
        name="ctx_attn",
    )(sink, pc["swa_q"], pc["swa_k"], pc["swa_v"], pc["na_q"], pc["na_k"], pc["na_v"])


def _outproj_kernel(yr_ref, yc_ref, ys_ref, yn_ref, w_ref, x_ref, g1_ref, gain_ref, sc_ref, sh_ref, wr_ref,
                    xo_ref, h_ref, lg_ref):
    gw = GROUP_WIDTH
    acc = _dot(yr_ref[0], w_ref[0:gw, :])
    acc += _dot(yc_ref[0], w_ref[gw:2 * gw, :])
    acc += _dot(ys_ref[0], w_ref[2 * gw:3 * gw, :])
    acc += _dot(yn_ref[0], w_ref[3 * gw:4 * gw, :])
    xn = x_ref[0] + g1_ref[0] * acc
    xo_ref[0] = xn
    hb = _rms_modulate(xn, gain_ref[...], sc_ref[0], sh_ref[0]).astype(BF16)
    h_ref[0] = hb
    lg_ref[0] = _dot(hb, wr_ref[...])


def _outproj(ys, w_out, x, g1, gain, sc, sh, w_router, *, tm):
    bsz, seq, d = x.shape
    yb = pl.BlockSpec((1, tm, GROUP_WIDTH), lambda b, i: (b, i, 0))
    xb = pl.BlockSpec((1, tm, d), lambda b, i: (b, i, 0))
    vec = pl.BlockSpec((1, 1, d), lambda b, i: (b, 0, 0))
    return pl.pallas_call(
        _outproj_kernel,
        grid=(bsz, seq // tm),
        in_specs=[yb, yb, yb, yb,
                  pl.BlockSpec((d, d), lambda b, i: (0, 0), pipeline_mode=pl.Buffered(1)),
                  xb, vec, pl.BlockSpec((1, d), lambda b, i: (0, 0)), vec, vec,
                  pl.BlockSpec((d, N_EXPERTS), lambda b, i: (0, 0))],
        out_specs=[xb, xb, pl.BlockSpec((1, tm, N_EXPERTS), lambda b, i: (b, i, 0))],
        out_shape=[jax.ShapeDtypeStruct((bsz, seq, d), F32), jax.ShapeDtypeStruct((bsz, seq, d), BF16),
                   jax.ShapeDtypeStruct((bsz, seq, N_EXPERTS), F32)],
        compiler_params=_params(("parallel", "parallel"), 48),
        name="outproj",
    )(*ys, w_out, x, g1, gain.reshape(1, d), sc, sh, w_router.astype(BF16))


def _cumsum_lanes(x, tri):
    outs = []
    carry = jnp.zeros((x.shape[0], 1), F32)
    for c0 in range(0, x.shape[1], 128):
        cs = _dot(x[:, c0:c0 + 128], tri) + carry
        outs.append(cs)
        carry = cs[:, 127:128]
    return jnp.concatenate(outs, axis=1) if len(outs) > 1 else outs[0]


def _router_kernel(lg_ref, pos_ref, gate_ref, *, cap):
    lg = lg_ref[0]
    e = jnp.exp(lg - jnp.max(lg, axis=0, keepdims=True))
    aff = e / jnp.sum(e, axis=0, keepdims=True)
    capf = float(cap)
    lo = jnp.zeros((lg.shape[0], 1), jnp.int32)
    for bit in range(30, -1, -1):
        cand = lo | (1 << bit)
        cnt = jnp.sum(jnp.where(aff >= pltpu.bitcast(cand, F32), 1.0, 0.0), axis=1, keepdims=True)
        lo = jnp.where(cnt >= capf, cand, lo)
    kth = pltpu.bitcast(lo, F32)
    gt = aff > kth
    eq = aff == kth
    need = capf - jnp.sum(jnp.where(gt, 1.0, 0.0), axis=1, keepdims=True)
    ri = lax.broadcasted_iota(jnp.int32, (128, 128), 0)
    ci = lax.broadcasted_iota(jnp.int32, (128, 128), 1)
    tri = jnp.where(ri <= ci, 1.0, 0.0).astype(BF16)
    eq_rank = _cumsum_lanes(jnp.where(eq, 1.0, 0.0).astype(BF16), tri)
    sel = gt | (eq & (eq_rank <= need))
    slot = _cumsum_lanes(jnp.where(sel, 1.0, 0.0).astype(BF16), tri) - 1.0
    pos_ref[0] = jnp.where(sel, slot, -1.0).astype(jnp.int32)
    gate_ref[0] = aff


def _router(logits_t, cap):
    bsz, n_e, t = logits_t.shape
    blk = pl.BlockSpec((1, n_e, t), lambda b: (b, 0, 0))
    return pl.pallas_call(
        functools.partial(_router_kernel, cap=cap),
        grid=(bsz,),
        in_specs=[blk],
        out_specs=[blk, blk],
        out_shape=[jax.ShapeDtypeStruct((bsz, n_e, t), jnp.int32), jax.ShapeDtypeStruct((bsz, n_e, t), F32)],
        compiler_params=_params(("parallel",), 32),
        name="router",
    )(logits_t)


EXPERT_ROWS = 256


def _expert_rows(pos_ref, gate_ref, h_ref, o_ref, wg, wu, wd, cap):
    nb, t = h_ref.shape[0], h_ref.shape[1]
    slot_id = lax.broadcasted_iota(jnp.int32, (cap, t), 0)
    rows, gates = [], []
    for i in range(nb):
        hit = slot_id == pos_ref[i, 0]
        rows.append(_dot(jnp.where(hit, 1.0, 0.0).astype(BF16), h_ref[i]).astype(BF16))
        gates.append(jnp.sum(jnp.where(hit, gate_ref[i, 0], 0.0), axis=1, keepdims=True))
    xs = jnp.concatenate(rows, axis=0) if nb > 1 else rows[0]
    gate = jnp.concatenate(gates, axis=0) if nb > 1 else gates[0]
    a = _dot(xs, wg)
    u = _dot(xs, wu)
    ye = (_dot((_silu(a) * u).astype(BF16), wd) * gate).astype(o_ref.dtype)
    for i in range(nb):
        o_ref[i, 0] = ye[i * cap:(i + 1) * cap]


def _expert_kernel(pos_ref, gate_ref, h_ref, wg_ref, wu_ref, wd_ref, o_ref, *rest, cap, n_e, emit):
    g, c = pl.program_id(0), pl.program_id(1)
    wg_s, wu_s, wd_s = rest[-3:]

    @pl.when(g < n_e)
    def _():
        slot = g % 2
        for w_ref, w_s, w_out in zip((wg_ref, wu_ref, wd_ref), (wg_s, wu_s, wd_s), rest[:-3] if emit else (None,) * 3):
            rows = w_ref.shape[0]
            chunk = w_ref[...].astype(BF16)
            w_s[slot, pl.ds(pl.multiple_of(c * rows, rows), rows), :] = chunk
            if w_out is not None:
                w_out[...] = chunk

    @pl.when(g > 0)
    def _():
        cur = (g - 1) % 2
        _expert_rows(pos_ref, gate_ref, h_ref, o_ref, wg_s[cur], wu_s[cur], wd_s[cur], cap)


def _experts(pos, gate, h, wg, wu, wd, layer, cap, emit):
    bsz, t, d = h.shape
    _, n_e, _, ff = wg.shape
    nc = bsz
    assert cap == EXPERT_ROWS and d % nc == 0 and ff % nc == 0

    def sample(g, c):
        return jnp.where(g == 0, 0, c)

    def act(g, c):
        return (sample(g, c), jnp.maximum(g - 1, 0), 0, 0)

    def wchunk(g, c):
        return (layer, jnp.minimum(g, n_e - 1), jnp.where(g < n_e, c, nc - 1), 0)

    def wemit(g, c):
        return wchunk(g, c)[1:]

    row = pl.BlockSpec((1, 1, 1, t), act)
    out_specs = [pl.BlockSpec((1, 1, cap, d), act)]
    out_shape = [jax.ShapeDtypeStruct((bsz, n_e, cap, d), BF16)]
    if emit:
        out_specs += [pl.BlockSpec((None, d // nc, ff), wemit), pl.BlockSpec((None, d // nc, ff), wemit),
                      pl.BlockSpec((None, ff // nc, d), wemit)]
        out_shape += [jax.ShapeDtypeStruct((n_e, d, ff), BF16), jax.ShapeDtypeStruct((n_e, d, ff), BF16),
                      jax.ShapeDtypeStruct((n_e, ff, d), BF16)]
    outs = pl.pallas_call(
        functools.partial(_expert_kernel, cap=cap, n_e=n_e, emit=emit),
        grid=(n_e + 1, nc),
        in_specs=[row, row, pl.BlockSpec((1, t, d), lambda g, c: (sample(g, c), 0, 0)),
                  pl.BlockSpec((None, None, d // nc, ff), wchunk),
                  pl.BlockSpec((None, None, d // nc, ff), wchunk),
                  pl.BlockSpec((None, None, ff // nc, d), wchunk)],
        out_specs=out_specs,
        out_shape=out_shape,
        scratch_shapes=[pltpu.VMEM((2, d, ff), BF16), pltpu.VMEM((2, d, ff), BF16), pltpu.VMEM((2, ff, d), BF16)],
        compiler_params=_params(("arbitrary", "arbitrary"), 58),
        name="experts",
    )(pos.reshape(bsz, n_e, 1, t), gate.reshape(bsz, n_e, 1, t), h, wg, wu, wd)
    return outs[0], tuple(outs[1:])


def _expert_small_kernel(pos_ref, gate_ref, h_ref, wg_ref, wu_ref, wd_ref, o_ref, *, cap):
    _expert_rows(pos_ref, gate_ref, h_ref, o_ref, wg_ref[...], wu_ref[...], wd_ref[...], cap)


def _experts_small(pos, gate, h, wg, wu, wd, cap):
    bsz, t, d = h.shape
    n_e, _, ff = wg.shape
    nb = min(bsz, EXPERT_ROWS // cap)
    assert bsz % nb == 0
    row = pl.BlockSpec((nb, 1, 1, t), lambda e, b: (b, e, 0, 0))
    return pl.pallas_call(
        functools.partial(_expert_small_kernel, cap=cap),
        grid=(n_e, bsz // nb),
        in_specs=[row, row, pl.BlockSpec((nb, t, d), lambda e, b: (b, 0, 0)),
                  pl.BlockSpec((None, d, ff), lambda e, b: (e, 0, 0)),
                  pl.BlockSpec((None, d, ff), lambda e, b: (e, 0, 0)),
                  pl.BlockSpec((None, ff, d), lambda e, b: (e, 0, 0))],
        out_specs=pl.BlockSpec((nb, 1, cap, d), lambda e, b: (b, e, 0, 0)),
        out_shape=jax.ShapeDtypeStruct((bsz, n_e, cap, d), BF16),
        compiler_params=_params(("arbitrary", "arbitrary"), 56),
        name="experts_small",
    )(pos.reshape(bsz, n_e, 1, t), gate.reshape(bsz, n_e, 1, t), h, wg, wu, wd)


def _combine_kernel(pos_ref, ye_ref, x_ref, g2_ref, *rest, cap):
    o_ref = rest[-1]
    n_e = pos_ref.shape[1]
    tb = x_ref.shape[1]
    slot = lax.broadcasted_iota(jnp.int32, (cap, tb), 0)
    hit = jnp.concatenate([jnp.where(slot == pos_ref[0, e:e + 1, :], 1.0, 0.0).astype(BF16) for e in range(n_e)],
                          axis=0)
    xn = x_ref[0] + g2_ref[0] * _dot_tn(hit, ye_ref[0])
    if len(rest) == 2:
        ms = jnp.mean(xn * xn, axis=-1, keepdims=True)
        xn = (xn * lax.rsqrt(ms + EPS)) * rest[0][...]
    o_ref[0] = xn


def _combine(pos, ye, x, g2, cap, tb, final_gain=None):
    bsz, t, d = x.shape
    n_e = pos.shape[1]
    xb = pl.BlockSpec((1, tb, d), lambda b, i: (b, i, 0))
    in_specs = [pl.BlockSpec((1, n_e, tb), lambda b, i: (b, 0, i)),
                pl.BlockSpec((1, n_e * cap, d), lambda b, i: (b, 0, 0)),
                xb, pl.BlockSpec((1, 1, d), lambda b, i: (b, 0, 0))]
    args = [pos, ye.reshape(bsz, n_e * cap, d), x, g2]
    if final_gain is not None:
        in_specs.append(pl.BlockSpec((1, d), lambda b, i: (0, 0)))
        args.append(final_gain.reshape(1, d))
    return pl.pallas_call(
        functools.partial(_combine_kernel, cap=cap),
        grid=(bsz, t // tb),
        in_specs=in_specs,
        out_specs=xb,
        out_shape=jax.ShapeDtypeStruct((bsz, t, d), F32),
        compiler_params=_params(("parallel", "arbitrary"), 56),
        name="combine",
    )(*args)


def _moe(x, h, logits, g2, lw, final_gain=None, emit=False, weights_bf16=None):
    bsz, t, d = x.shape
    cap = EC_CAPACITY_FACTOR * t // N_EXPERTS
    pos, gate = _router(jnp.swapaxes(logits, 1, 2), cap)
    if weights_bf16 is None:
        ye, emitted = _experts(pos, gate, h, lw["w_gate"], lw["w_up"], lw["w_down"], lw["layer"], cap, emit)
    else:
        ye, emitted = _experts_small(pos, gate, h, *weights_bf16, cap), ()
    return _combine(pos, ye, x, g2, cap, min(t, 512), final_gain), emitted


def _layer(x, xc, mods, mods_c, lw, need_ctx, final_gain):
    sh1, sc1, g1, sh2, sc2, g2 = mods
    sh1c, sc1c, g1c, sh2c, sc2c, g2c = mods_c
    names_lat = tuple(p[0] for p in _PIECES)
    p = _inproj(x, lw["norm_mix"], sc1, sh1, lw["w_in"], rope=True, names=names_lat, tm=TOKEN_TILE)
    pc = _inproj(xc, lw["norm_mix"], sc1c, sh1c, lw["w_in"], rope=False,
                 names=names_lat if need_ctx else _CTX_KV_ONLY, tm=CTX_TOKEN_TILE)

    y_ret, yc_ret = _retention(lw["ret_decay"], p, pc, need_ctx)
    y_conv = _short_conv(lw["conv_w"], p["conv_b"], p["conv_c"], p["conv_h"])
    y_swa = _swa(lw["swa_sink"], p, pc)
    y_na = _na(_na_bias(lw["na_rpb"]), p, pc)
    x, h, logits = _outproj((y_ret, y_conv, y_swa, y_na), lw["w_out"], x, g1, lw["norm_ffn"], sc2, sh2,
                            lw["w_router"], tm=TOKEN_TILE)
    x, w_bf16 = _moe(x, h, logits, g2, lw, final_gain, emit=need_ctx)
    if need_ctx:
        yc_conv = _short_conv(lw["conv_w"], pc["conv_b"], pc["conv_c"], pc["conv_h"])
        yc_swa, yc_na = _ctx_attn(lw["swa_sink"], pc)
        xc, hc, logits_c = _outproj((yc_ret, yc_conv, yc_swa, yc_na), lw["w_out"], xc, g1c, lw["norm_ffn"],
                                    sc2c, sh2c, lw["w_router"], tm=CTX_TOKEN_TILE)
        xc, _ = _moe(xc, hc, logits_c, g2c, lw, weights_bf16=w_bf16)
    return x, xc


def kernel(x, c, ctx, c_ctx, w_ada, b_ada, norm_mix, norm_ffn, w_in, w_out, ret_decay_fwd, ret_decay_bwd,
           conv_w, swa_sink, na_rpb, w_router, w_gate, w_up, w_down, norm_final):
    bsz, _, d = x.shape
    depth = w_ada.shape[0]
    assert bsz + 1 <= ADA_ROWS
    cs = jnp.zeros((ADA_ROWS, d), F32).at[:bsz].set(c).at[bsz].set(c_ctx)
    ada = _ada(cs, w_ada, b_ada)
    w_out_b = w_out.astype(BF16)
    xc = ctx
    for l in range(depth):
        mods = tuple(ada[l, :bsz, i * d:(i + 1) * d].reshape(bsz, 1, d) for i in range(6))
        mods_c = tuple(jnp.broadcast_to(ada[l, bsz, i * d:(i + 1) * d].reshape(1, 1, d), (bsz, 1, d))
                       for i in range(6))
        lw = dict(norm_mix=norm_mix[l], norm_ffn=norm_ffn[l], w_in=w_in[l].astype(BF16),
                  w_out=w_out_b[l], ret_decay=jnp.stack([ret_decay_fwd[l], ret_decay_bwd[l]]), conv_w=conv_w[l],
                  swa_sink=swa_sink[l], na_rpb=na_rpb[l], w_router=w_router[l],
                  w_gate=w_gate, w_up=w_up, w_down=w_down, layer=l)
        last = l == depth - 1
        x, xc = _layer(x, xc, mods, mods_c, lw, need_ctx=not last, final_gain=norm_final if last else None)
    return x
```

```python
import functools

import numpy as np
import jax
import jax.numpy as jnp
from jax import lax
from jax.experimental import pallas as pl
from jax.experimental.pallas import tpu as pltpu

D_MODEL = 2048
DEPTH = 2
GRID_W = 64
HEAD_DIM = 64
GROUP_WIDTH = D_MODEL // 4
RET_CHUNK = 128
SWA_HEADS = 8
SWA_KV_HEADS = 2
SWA_WINDOW = 128
SWA_BLOCK = 128
NA_HEADS = 8
NA_ROWS = 8
NA_COLS = 16
N_EXPERTS = 16
EXPERT_FF = D_MODEL // 2
EC_CAPACITY_FACTOR = 2
ROPE_BASE = 10000.0
EPS = 1e-6
NEG_INF = -1e30
F32 = jnp.float32
BF16 = jnp.bfloat16
ADA_ROWS = 16
QK_SCALE = HEAD_DIM ** -0.5
LOG2E = 1.4426950408889634
PAIR = 2 * HEAD_DIM
TOKEN_TILE = 512
OUT_TOKEN_TILE = 256
CTX_TOKEN_TILE = 256

_PIECES = (
    ("ret_q", 0, 512, True, 1.0, False),
    ("ret_k", 512, 512, True, QK_SCALE, False),
    ("ret_v", 1024, 512, False, 1.0, False),
    ("ret_g", 1536, 512, False, 1.0, False),
    ("conv_b", 2048, 512, False, 1.0, False),
    ("conv_c", 2560, 512, False, 1.0, False),
    ("conv_h", 3072, 512, False, 1.0, False),
    ("swa_q", 3584, 512, True, QK_SCALE * LOG2E, False),
    ("swa_k", 4096, 128, True, 1.0, True),
    ("swa_v", 4224, 128, False, 1.0, True),
    ("na_q", 4352, 512, False, QK_SCALE * LOG2E, False),
    ("na_k", 4864, 512, False, 1.0, False),
    ("na_v", 5376, 512, False, 1.0, False),
)
IN_COLS = 5888
_CTX_KV_ONLY = ("ret_k", "ret_v", "swa_k", "swa_v", "na_k", "na_v")


def _params(sem, vmem_mb):
    return pltpu.CompilerParams(dimension_semantics=sem, vmem_limit_bytes=vmem_mb << 20)


def _silu(x):
    return x * jax.nn.sigmoid(x)


def _dot(a, b):
    return jnp.dot(a, b, preferred_element_type=F32)


def _dot_nt(a, b):
    return lax.dot_general(a, b, (((1,), (1,)), ((), ())), preferred_element_type=F32)


def _dot_tn(a, b):
    return lax.dot_general(a, b, (((0,), (0,)), ((), ())), preferred_element_type=F32)


def _first_head_lanes():
    return lax.broadcasted_iota(jnp.int32, (1, PAIR), 1) < HEAD_DIM


def _keep_head(x, lo, j):
    return jnp.where(lo if j == 0 else jnp.logical_not(lo), x, jnp.zeros_like(x))


def _ada_kernel(c_ref, w_ref, b_ref, o_ref):
    s = _silu(c_ref[...]).astype(BF16)
    o_ref[0] = _dot(s, w_ref[0].astype(BF16)) + b_ref[0]


def _ada(cs, w_ada, b_ada):
    depth, d, n = w_ada.shape
    tn = 1024
    return pl.pallas_call(
        _ada_kernel,
        grid=(depth, n // tn),
        in_specs=[
            pl.BlockSpec((ADA_ROWS, d), lambda l, j: (0, 0)),
            pl.BlockSpec((1, d, tn), lambda l, j: (l, 0, j)),
            pl.BlockSpec((1, 1, tn), lambda l, j: (l, 0, j)),
        ],
        out_specs=pl.BlockSpec((1, ADA_ROWS, tn), lambda l, j: (l, 0, j)),
        out_shape=jax.ShapeDtypeStruct((depth, ADA_ROWS, n), F32),
        compiler_params=_params(("arbitrary", "arbitrary"), 40),
        name="ada",
    )(cs, w_ada, b_ada.reshape(depth, 1, n))


def _rope_tables(seq):
    half = HEAD_DIM // 2
    nf = half // 2
    t = np.arange(seq)
    row = (t // GRID_W).astype(np.float32)
    col = (t % GRID_W).astype(np.float32)
    inv = (np.float32(ROPE_BASE) ** (-np.arange(nf, dtype=np.float32) / np.float32(nf))).astype(np.float32)
    ang = np.concatenate([row[:, None] * inv, col[:, None] * inv], axis=-1).astype(np.float32)
    cos, sin = np.cos(ang), np.sin(ang)
    cos64 = np.concatenate([cos, cos], axis=-1)
    sin64 = np.concatenate([-sin, sin], axis=-1)
    return (np.tile(cos64, (1, 2)).astype(np.float32), np.tile(sin64, (1, 2)).astype(np.float32))


def _rope(z, cos, sin_signed):
    w = z.shape[1]
    reps = w // 128
    cos_w = jnp.concatenate([cos] * reps, axis=1) if reps > 1 else cos
    sin_w = jnp.concatenate([sin_signed] * reps, axis=1) if reps > 1 else sin_signed
    lane = lax.broadcasted_iota(jnp.int32, (1, w), 1)
    first_half = (lane & (HEAD_DIM - 1)) < (HEAD_DIM // 2)
    up = pltpu.roll(z, w - HEAD_DIM // 2, 1)
    dn = pltpu.roll(z, HEAD_DIM // 2, 1)
    return z * cos_w + jnp.where(first_half, up, dn) * sin_w


def _rms_modulate(x, gain, scale, shift):
    ms = jnp.mean(x * x, axis=-1, keepdims=True)
    return (x * lax.rsqrt(ms + EPS)) * (gain * (1.0 + scale)) + shift


def _inproj_kernel(*refs, pieces, rope):
    if rope:
        x_ref, gain_ref, sc_ref, sh_ref, cos_ref, sin_ref, w_ref = refs[:7]
        o_refs = refs[7:]
    else:
        x_ref, gain_ref, sc_ref, sh_ref, w_ref = refs[:5]
        o_refs = refs[5:]
    hb = _rms_modulate(x_ref[0], gain_ref[...], sc_ref[0], sh_ref[0]).astype(BF16)
    for (_, off, width, do_rope, scale, dup), o_ref in zip(pieces, o_refs):
        z = _dot(hb, w_ref[:, off:off + width])
        if rope and do_rope:
            z = _rope(z, cos_ref[...], sin_ref[...])
        if scale != 1.0:
            z = z * scale
        if dup:
            heads = [z[:, j * HEAD_DIM:(j + 1) * HEAD_DIM] for j in range(width // HEAD_DIM)]
            z = jnp.concatenate([h for h in heads for _ in range(2)], axis=1)
        o_ref[0] = z.astype(BF16)


def _inproj(x, gain, sc, sh, w_in, *, rope, names, tm):
    bsz, seq, d = x.shape
    pieces = tuple(p for p in _PIECES if p[0] in names)
    widths = [p[2] * (2 if p[5] else 1) for p in pieces]
    vec = pl.BlockSpec((1, 1, d), lambda b, i: (b, 0, 0))
    in_specs = [pl.BlockSpec((1, tm, d), lambda b, i: (b, i, 0)),
                pl.BlockSpec((1, d), lambda b, i: (0, 0)), vec, vec]
    args = [x, gain.reshape(1, d), sc, sh]
    if rope:
        cos, sin = _rope_tables(seq)
        in_specs += [pl.BlockSpec((tm, 128), lambda b, i: (i, 0))] * 2
        args += [jnp.asarray(cos), jnp.asarray(sin)]
    in_specs.append(pl.BlockSpec((d, IN_COLS), lambda b, i: (0, 0), pipeline_mode=pl.Buffered(1)))
    args.append(w_in)
    outs = pl.pallas_call(
        functools.partial(_inproj_kernel, pieces=pieces, rope=rope),
        grid=(bsz, seq // tm),
        in_specs=in_specs,
        out_specs=[pl.BlockSpec((1, tm, w), lambda b, i: (b, i, 0)) for w in widths],
        out_shape=[jax.ShapeDtypeStruct((bsz, seq, w), BF16) for w in widths],
        compiler_params=_params(("parallel", "parallel"), 56),
        name="inproj_lat" if rope else "inproj_ctx",
    )(*args)
    return {p[0]: o for p, o in zip(pieces, outs)}


def _log_sigmoid(x):
    return jnp.minimum(x, 0.0) - jnp.log(1.0 + jnp.exp(-jnp.abs(x)))


RET_PAIRS = 2


def _ret_kernel(dec_ref, q_ref, k_ref, v_ref, g_ref, qc_ref, kc_ref, vc_ref, gc_ref, *rest, need_ctx):
    if need_ctx:
        y_ref, yc_ref, sb_ref = rest
    else:
        (y_ref, sb_ref), yc_ref = rest, None
    c = RET_CHUNK
    hd = HEAD_DIM
    n_lat = q_ref.shape[1] // c
    n_ctx = qc_ref.shape[1] // c
    n_pairs = q_ref.shape[2] // PAIR
    first_head = 2 * n_pairs * pl.program_id(1)

    lo = _first_head_lanes()
    ri = lax.broadcasted_iota(jnp.int32, (c, c), 0)
    ci = lax.broadcasted_iota(jnp.int32, (c, c), 1)
    diff = (ri - ci).astype(F32)
    same_head = (ri < hd) == (ci < hd)
    pos = lax.broadcasted_iota(jnp.int32, (c, PAIR), 0).astype(F32)

    consts = []
    for p in range(n_pairs):
        lg_f = [_log_sigmoid(jnp.full((1, c), dec_ref[0, first_head + 2 * p + j], F32)) for j in range(2)]
        lg_b = [_log_sigmoid(jnp.full((1, c), dec_ref[1, first_head + 2 * p + j], F32)) for j in range(2)]
        dmat = jnp.concatenate(
            [jnp.where(diff >= 0.0, jnp.exp(lg_f[j] * jnp.maximum(diff, 0.0)),
                       jnp.exp(lg_b[j] * jnp.maximum(-diff, 0.0))) for j in range(2)], axis=0)
        lf = jnp.where(lo, lg_f[0], lg_f[1])
        lb = jnp.where(lo, lg_b[0], lg_b[1])
        consts.append(dict(dmat=dmat, xi_f=jnp.exp(lf * (pos + 1.0)), xi_b=jnp.exp(lb * (c - pos)),
                           ze_f=jnp.exp(lf * (c - 1.0 - pos)), ze_b=jnp.exp(lb * pos),
                           cd_f=jnp.exp(lf * float(c)), cd_b=jnp.exp(lb * float(c))))

    def rows(ref, r0, p):
        return ref[0, pl.ds(r0, c), p * PAIR:(p + 1) * PAIR]

    def kv_outer(k, v, zeta):
        kz = (k.astype(F32) * zeta).astype(BF16)
        return jnp.where(same_head, _dot_tn(kz, v), 0.0)

    def half_sum(x):
        s_lo = jnp.sum(jnp.where(lo, x, 0.0), axis=-1, keepdims=True)
        s_all = jnp.sum(x, axis=-1, keepdims=True)
        return jnp.where(lo, s_lo, s_all - s_lo)

    def bwd_chunk(k_r, v_r, r0, slot, states):
        new = []
        for p in range(n_pairs):
            sb_ref[slot, p] = states[p]
            new.append(states[p] * consts[p]["cd_b"] + kv_outer(rows(k_r, r0, p), rows(v_r, r0, p), consts[p]["ze_b"]))
        return tuple(new)

    def fwd_chunk(q_r, k_r, v_r, g_r, o_r, r0, slot, states):
        ks = [rows(k_r, r0, p) for p in range(n_pairs)]
        vs = [rows(v_r, r0, p) for p in range(n_pairs)]
        if o_r is not None:
            qs = [rows(q_r, r0, p) for p in range(n_pairs)]
            decayed = []
            for p in range(n_pairs):
                q2 = jnp.concatenate([_keep_head(qs[p], lo, 0), _keep_head(qs[p], lo, 1)], axis=0)
                decayed.append((_dot_nt(q2, ks[p]) * consts[p]["dmat"]).astype(BF16))
            ys = []
            for p in range(n_pairs):
                cp, a, qf = consts[p], decayed[p], qs[p].astype(F32)
                lhs = jnp.concatenate([a[:c], a[c:], (qf * cp["xi_f"]).astype(BF16), (qf * cp["xi_b"]).astype(BF16)],
                                      axis=1)
                rhs = jnp.concatenate([_keep_head(vs[p], lo, 0), _keep_head(vs[p], lo, 1), states[p].astype(BF16),
                                       sb_ref[slot, p].astype(BF16)], axis=0)
                ys.append(_dot(lhs, rhs))
            for p in range(n_pairs):
                yc = ys[p] - half_sum(ys[p]) * (1.0 / hd)
                var = half_sum(yc * yc) * (1.0 / hd)
                out = _silu(rows(g_r, r0, p).astype(F32)) * (yc * lax.rsqrt(var + EPS))
                o_r[0, pl.ds(r0, c), p * PAIR:(p + 1) * PAIR] = out.astype(o_r.dtype)
        return tuple(states[p] * consts[p]["cd_f"] + kv_outer(ks[p], vs[p], consts[p]["ze_f"])
                     for p in range(n_pairs))

    zero = tuple(jnp.zeros((PAIR, PAIR), F32) for _ in range(n_pairs))
    st = zero
    for cc in range(n_ctx - 1, -1, -1):
        st = bwd_chunk(kc_ref, vc_ref, cc * c, cc, st)

    def bwd_body(i, s):
        cl = n_lat - 1 - i
        return bwd_chunk(k_ref, v_ref, pl.multiple_of(cl * c, c), n_ctx + cl, s)

    lax.fori_loop(0, n_lat, bwd_body, st, unroll=4)

    st = zero
    for cc in range(n_ctx):
        st = fwd_chunk(qc_ref, kc_ref, vc_ref, gc_ref, yc_ref, cc * c, cc, st)

    def fwd_body(cl, s):
        return fwd_chunk(q_ref, k_ref, v_ref, g_ref, y_ref, pl.multiple_of(cl * c, c), n_ctx + cl, s)

    lax.fori_loop(0, n_lat, fwd_body, st, unroll=4)


def _retention(dec, p, pc, need_ctx):
    bsz, seq, w = p["ret_q"].shape
    t = pc["ret_k"].shape[1]
    bw = RET_PAIRS * PAIR
    lat = pl.BlockSpec((1, seq, bw), lambda b, h: (b, 0, h))
    ctx = pl.BlockSpec((1, t, bw), lambda b, h: (b, 0, h))
    out_specs, out_shape = [lat], [jax.ShapeDtypeStruct((bsz, seq, w), BF16)]
    if need_ctx:
        out_specs.append(ctx)
        out_shape.append(jax.ShapeDtypeStruct((bsz, t, w), BF16))
        qc, gc = pc["ret_q"], pc["ret_g"]
    else:
        qc, gc = pc["ret_k"], pc["ret_k"]
    n_chunks = (seq + t) // RET_CHUNK
    outs = pl.pallas_call(
        functools.partial(_ret_kernel, need_ctx=need_ctx),
        grid=(bsz, w // bw),
        in_specs=[pl.BlockSpec(memory_space=pltpu.SMEM), lat, lat, lat, lat, ctx, ctx, ctx, ctx],
        out_specs=out_specs,
        out_shape=out_shape,
        scratch_shapes=[pltpu.VMEM((n_chunks, RET_PAIRS, PAIR, PAIR), F32)],
        compiler_params=_params(("parallel", "parallel"), 32),
        name="retention",
    )(dec, p["ret_q"], p["ret_k"], p["ret_v"], p["ret_g"], qc, pc["ret_k"], pc["ret_v"], gc)
    return (outs[0], outs[1]) if need_ctx else (outs[0], None)


def _conv_kernel(w_ref, b_ref, c_ref, h_ref, o_ref):
    seq = b_ref.shape[1]
    rows = min(256, seq)
    halo = 16
    w0, w1, w2 = w_ref[0:1, :], w_ref[1:2, :], w_ref[2:3, :]
    for r0 in range(0, seq, rows):
        lo, hi = max(0, r0 - halo), min(seq, r0 + rows + halo)
        u = c_ref[0, lo:hi, :].astype(F32) * h_ref[0, lo:hi, :].astype(F32)
        n = hi - lo
        t = lo + lax.broadcasted_iota(jnp.int32, (n, 1), 0)
        prev = jnp.where(t == 0, 0.0, pltpu.roll(u, 1, 0))
        nxt = jnp.where(t == seq - 1, 0.0, pltpu.roll(u, n - 1, 0))
        y = prev * w0 + u * w1 + nxt * w2
        y = y[r0 - lo:r0 - lo + rows]
        o_ref[0, r0:r0 + rows, :] = (b_ref[0, r0:r0 + rows, :].astype(F32) * y).astype(o_ref.dtype)


def _short_conv(conv_w, b, c, h):
    bsz, seq, w = b.shape
    blk = pl.BlockSpec((1, seq, w), lambda i: (i, 0, 0))
    return pl.pallas_call(
        _conv_kernel,
        grid=(bsz,),
        in_specs=[pl.BlockSpec((3, w), lambda i: (0, 0)), blk, blk, blk],
        out_specs=blk,
        out_shape=jax.ShapeDtypeStruct((bsz, seq, w), BF16),
        compiler_params=_params(("parallel",), 48),
        name="short_conv",
    )(conv_w, b, c, h)


def _softmax_pv_t(scores_t, values, extra=None):
    mx = functools.reduce(jnp.maximum, [jnp.max(s, axis=0, keepdims=True) for s in scores_t])
    if extra is not None:
        mx = jnp.maximum(mx, extra)
    den = jnp.exp2(extra - mx) if extra is not None else 0.0
    acc = 0.0
    for s, v in zip(scores_t, values):
        p = jnp.exp2(s - mx)
        den = den + jnp.sum(p, axis=0, keepdims=True)
        acc = acc + _dot_tn(v, p.astype(BF16))
    return (acc / den).T


def _swa_scores(q_ref, j, lo, win_keys, valid_t, ctx_keys):
    grp = SWA_HEADS // SWA_KV_HEADS
    qs = []
    for pp in range(grp // 2):
        qp = q_ref[0, :, (j * grp // 2 + pp) * PAIR:(j * grp // 2 + pp + 1) * PAIR]
        qs += [_keep_head(qp, lo, 0), _keep_head(qp, lo, 1)]
    qg = jnp.concatenate(qs, axis=0)
    scores = []
    if win_keys is not None:
        scores.append(jnp.where(valid_t, _dot_nt(win_keys, qg), NEG_INF))
    scores.append(_dot_nt(ctx_keys, qg))
    return scores


def _swa_outputs(scores, values, sink_ref, j, lo, m):
    grp = SWA_HEADS // SWA_KV_HEADS
    sink = jnp.concatenate([jnp.full((1, m), sink_ref[j * grp + g] * LOG2E, F32) for g in range(grp)], axis=1)
    o = _softmax_pv_t(scores, values, sink)
    return [jnp.where(lo, o[2 * pp * m:(2 * pp + 1) * m], o[(2 * pp + 1) * m:(2 * pp + 2) * m])
            for pp in range(grp // 2)]


def _swa_kernel(sink_ref, q_ref, k_ref, v_ref, kc_ref, vc_ref, o_ref):
    blk = SWA_BLOCK
    win = 3 * blk
    seq = k_ref.shape[1]
    grp = SWA_HEADS // SWA_KV_HEADS
    i = pl.program_id(1)
    start = pl.multiple_of(jnp.clip((i - 1) * blk, 0, seq - win), blk)
    lo = _first_head_lanes()
    kpos = start + lax.broadcasted_iota(jnp.int32, (win, blk), 0)
    qpos = i * blk + lax.broadcasted_iota(jnp.int32, (win, blk), 1)
    valid = jnp.abs(kpos - qpos) <= SWA_WINDOW
    valid = jnp.concatenate([valid] * grp, axis=1)
    kv_lanes = [slice(j * PAIR, (j + 1) * PAIR) for j in range(SWA_KV_HEADS)]
    scores = [_swa_scores(q_ref, j, lo, k_ref[0, pl.ds(start, win), sl], valid, kc_ref[0, :, sl])
              for j, sl in enumerate(kv_lanes)]
    for j, sl in enumerate(kv_lanes):
        outs = _swa_outputs(scores[j], [v_ref[0, pl.ds(start, win), sl], vc_ref[0, :, sl]], sink_ref, j, lo, blk)
        for pp, o in enumerate(outs):
            o_ref[0, :, (j * grp // 2 + pp) * PAIR:(j * grp // 2 + pp + 1) * PAIR] = o.astype(o_ref.dtype)


def _swa(sink, p, pc):
    bsz, seq, w = p["swa_q"].shape
    t = pc["swa_k"].shape[1]
    kvw = SWA_KV_HEADS * PAIR
    kv = pl.BlockSpec((1, seq, kvw), lambda b, i: (b, 0, 0))
    kvc = pl.BlockSpec((1, t, kvw), lambda b, i: (b, 0, 0))
    qb = pl.BlockSpec((1, SWA_BLOCK, w), lambda b, i: (b, i, 0))
    return pl.pallas_call(
        _swa_kernel,
        grid=(bsz, seq // SWA_BLOCK),
        in_specs=[pl.BlockSpec(memory_space=pltpu.SMEM), qb, kv, kv, kvc, kvc],
        out_specs=qb,
        out_shape=jax.ShapeDtypeStruct((bsz, seq, w), BF16),
        compiler_params=_params(("parallel", "parallel"), 32),
        name="swa",
    )(sink, p["swa_q"], p["swa_k"], p["swa_v"], pc["swa_k"], pc["swa_v"])


NA_GROUP = 4
NA_BAND = NA_ROWS + NA_GROUP - 1
NA_GRID_ROWS = 32
_NA_CFGS = ((0, 0), (4, 0), (8, 4), (28, 21))


def _na_bias_kernel(rpb_ref, o_ref):
    hp = pl.program_id(0)
    gw, wc = GRID_W, NA_COLS
    rows = NA_GRID_ROWS
    ck = lax.broadcasted_iota(jnp.int32, (gw, gw), 0)
    cq = lax.broadcasted_iota(jnp.int32, (gw, gw), 1)
    col_start = jnp.clip(cq - wc // 2, 0, gw - wc)
    col_ok = (ck >= col_start) & (ck < col_start + wc)
    dc = jnp.clip(ck - cq, -(wc - 1), wc - 1) + (wc - 1)
    neg = jnp.full((gw, gw), NEG_INF, F32)
    per_dr = []
    for j in range(2):
        tiles = []
        for dr in range(2 * NA_ROWS - 1):
            t = jnp.zeros((gw, gw), F32)
            for d in range(2 * wc - 1):
                t = jnp.where(dc == d, rpb_ref[2 * hp + j, dr, d] * LOG2E, t)
            tiles.append(jnp.where(col_ok, t, NEG_INF))
        per_dr.append(tiles)
    for c, (r0, start) in enumerate(_NA_CFGS):
        for jp in range(NA_BAND):
            krow = start + jp
            blocks = []
            for j in range(2):
                for i in range(NA_GROUP):
                    r = r0 + i
                    band0 = min(max(r - NA_ROWS // 2, 0), rows - NA_ROWS)
                    blocks.append(per_dr[j][krow - r + NA_ROWS - 1] if band0 <= krow < band0 + NA_ROWS else neg)
            o_ref[c, 0, jp * gw:(jp + 1) * gw, :] = jnp.concatenate(blocks, axis=1)


def _na_bias(rpb):
    n_cfg = len(_NA_CFGS)
    blk = (n_cfg, 1, NA_BAND * GRID_W, 2 * NA_GROUP * GRID_W)
    return pl.pallas_call(
        _na_bias_kernel,
        grid=(NA_HEADS // 2,),
        in_specs=[pl.BlockSpec(memory_space=pltpu.SMEM)],
        out_specs=pl.BlockSpec(blk, lambda h: (0, h, 0, 0)),
        out_shape=jax.ShapeDtypeStruct((n_cfg, NA_HEADS // 2) + blk[2:], F32),
        compiler_params=_params(("parallel",), 32),
        name="na_bias",
    )(rpb)


def _na_scores(qp, lo, keys, bias):
    q2 = jnp.concatenate([_keep_head(qp, lo, 0), _keep_head(qp, lo, 1)], axis=0)
    scores = [_dot_nt(k, q2) for k in keys]
    if bias is not None:
        scores[0] = scores[0] + bias
    return scores


def _na_pair(qp, lo, keys, values, bias):
    m = qp.shape[0]
    o = _softmax_pv_t(_na_scores(qp, lo, keys, bias), values)
    return jnp.where(lo, o[:m], o[m:])


def _na_kernel(q_ref, k_ref, v_ref, kc_ref, vc_ref, bias_ref, o_ref):
    rows = k_ref.shape[1] // GRID_W
    nk = NA_BAND * GRID_W
    g = pl.program_id(1)
    first = jnp.clip(g * NA_GROUP - NA_ROWS // 2, 0, rows - NA_BAND)
    st = pl.multiple_of(first * GRID_W, GRID_W)
    lo = _first_head_lanes()
    m = q_ref.shape[1]
    pairs = [slice(hp * PAIR, (hp + 1) * PAIR) for hp in range(NA_HEADS // 2)]
    scores = [_na_scores(q_ref[0, :, sl], lo, [k_ref[0, pl.ds(st, nk), sl], kc_ref[0, :, sl]], bias_ref[0, hp])
              for hp, sl in enumerate(pairs)]
    for hp, sl in enumerate(pairs):
        o = _softmax_pv_t(scores[hp], [v_ref[0, pl.ds(st, nk), sl], vc_ref[0, :, sl]])
        o_ref[0, :, sl] = jnp.where(lo, o[:m], o[m:]).astype(o_ref.dtype)


def _na(bias, p, pc):
    bsz, seq, w = p["na_q"].shape
    t = pc["na_k"].shape[1]
    rows = seq // GRID_W
    assert rows == NA_GRID_ROWS and bias.shape[0] == len(_NA_CFGS)
    n_groups = rows // NA_GROUP
    kv = pl.BlockSpec((1, seq, w), lambda b, g: (b, 0, 0))
    kvc = pl.BlockSpec((1, t, w), lambda b, g: (b, 0, 0))
    qb = pl.BlockSpec((1, NA_GROUP * GRID_W, w), lambda b, g: (b, g, 0))

    def cfg(b, g):
        return (jnp.where(g < 2, g, jnp.where(g < n_groups - 1, 2, 3)), 0, 0, 0)

    return pl.pallas_call(
        _na_kernel,
        grid=(bsz, n_groups),
        in_specs=[qb, kv, kv, kvc, kvc,
                  pl.BlockSpec((1, NA_HEADS // 2, NA_BAND * GRID_W, 2 * NA_GROUP * GRID_W), cfg)],
        out_specs=qb,
        out_shape=jax.ShapeDtypeStruct((bsz, seq, w), BF16),
        compiler_params=_params(("parallel", "arbitrary"), 48),
        name="na",
    )(p["na_q"], p["na_k"], p["na_v"], pc["na_k"], pc["na_v"], bias)


def _ctx_attn_kernel(sink_ref, sq_ref, sk_ref, sv_ref, nq_ref, nk_ref, nv_ref, so_ref, no_ref):
    grp = SWA_HEADS // SWA_KV_HEADS
    lo = _first_head_lanes()
    for j in range(SWA_KV_HEADS):
        sl = slice(j * PAIR, (j + 1) * PAIR)
        outs = _swa_outputs(_swa_scores(sq_ref, j, lo, None, None, sk_ref[0, :, sl]), [sv_ref[0, :, sl]], sink_ref, j,
                            lo, sq_ref.shape[1])
        for pp, o in enumerate(outs):
            so_ref[0, :, (j * grp // 2 + pp) * PAIR:(j * grp // 2 + pp + 1) * PAIR] = o.astype(so_ref.dtype)
    for hp in range(NA_HEADS // 2):
        sl = slice(hp * PAIR, (hp + 1) * PAIR)
        o = _na_pair(nq_ref[0, :, sl], lo, [nk_ref[0, :, sl]], [nv_ref[0, :, sl]], None)
        no_ref[0, :, sl] = o.astype(no_ref.dtype)


def _ctx_attn(sink, pc):
    bsz, t, w = pc["swa_q"].shape
    kvw = SWA_KV_HEADS * PAIR
    full = pl.BlockSpec((1, t, w), lambda b: (b, 0, 0))
    kv = pl.BlockSpec((1, t, kvw), lambda b: (b, 0, 0))
    return pl.pallas_call(
        _ctx_attn_kernel,
        grid=(bsz,),
        in_specs=[pl.BlockSpec(memory_space=pltpu.SMEM), full, kv, kv, full, full, full],
        out_specs=[full, full],
        out_shape=[jax.ShapeDtypeStruct((bsz, t, w), BF16)] * 2,
        compiler_params=_params(("parallel",), 32),
        name="ctx_attn",
    )(sink, pc["swa_q"], pc["swa_k"], pc["swa_v"], pc["na_q"], pc["na_k"], pc["na_v"])


def _outproj_kernel(yr_ref, yc_ref, ys_ref, yn_ref, w_ref, x_ref, g1_ref, gain_ref, sc_ref, sh_ref, wr_ref,
                    xo_ref, h_ref, lg_ref):
    y = jnp.concatenate([yr_ref[0], yc_ref[0], ys_ref[0], yn_ref[0]], axis=1)
    xn = x_ref[0] + g1_ref[0] * _dot(y, w_ref[...])
    xo_ref[0] = xn
    hb = _rms_modulate(xn, gain_ref[...], sc_ref[0], sh_ref[0]).astype(BF16)
    h_ref[0] = hb
    lg_ref[0] = _dot(hb, wr_ref[...])


def _outproj(ys, w_out, x, g1, gain, sc, sh, w_router, *, tm):
    bsz, seq, d = x.shape
    yb = pl.BlockSpec((1, tm, GROUP_WIDTH), lambda b, i: (b, i, 0))
    xb = pl.BlockSpec((1, tm, d), lambda b, i: (b, i, 0))
    vec = pl.BlockSpec((1, 1, d), lambda b, i: (b, 0, 0))
    return pl.pallas_call(
        _outproj_kernel,
        grid=(bsz, seq // tm),
        in_specs=[yb, yb, yb, yb,
                  pl.BlockSpec((d, d), lambda b, i: (0, 0), pipeline_mode=pl.Buffered(1)),
                  xb, vec, pl.BlockSpec((1, d), lambda b, i: (0, 0)), vec, vec,
                  pl.BlockSpec((d, N_EXPERTS), lambda b, i: (0, 0))],
        out_specs=[xb, xb, pl.BlockSpec((1, tm, N_EXPERTS), lambda b, i: (b, i, 0))],
        out_shape=[jax.ShapeDtypeStruct((bsz, seq, d), F32), jax.ShapeDtypeStruct((bsz, seq, d), BF16),
                   jax.ShapeDtypeStruct((bsz, seq, N_EXPERTS), F32)],
        compiler_params=_params(("parallel", "parallel"), 48),
        name="outproj",
    )(*ys, w_out, x, g1, gain.reshape(1, d), sc, sh, w_router.astype(BF16))


def _cumsum_lanes(x, tri):
    outs = []
    carry = jnp.zeros((x.shape[0], 1), F32)
    for c0 in range(0, x.shape[1], 128):
        cs = _dot(x[:, c0:c0 + 128], tri) + carry
        outs.append(cs)
        carry = cs[:, 127:128]
    return jnp.concatenate(outs, axis=1) if len(outs) > 1 else outs[0]


def _router_kernel(lg_ref, pos_ref, gate_ref, *, cap):
    lg = lg_ref[0]
    e = jnp.exp(lg - jnp.max(lg, axis=0, keepdims=True))
    aff = e / jnp.sum(e, axis=0, keepdims=True)
    capf = float(cap)
    lo = jnp.zeros((lg.shape[0], 1), jnp.int32)
    for bit in range(30, -1, -1):
        cand = lo | (1 << bit)
        cnt = jnp.sum(jnp.where(aff >= pltpu.bitcast(cand, F32), 1.0, 0.0), axis=1, keepdims=True)
        lo = jnp.where(cnt >= capf, cand, lo)
    kth = pltpu.bitcast(lo, F32)
    gt = aff > kth
    eq = aff == kth
    need = capf - jnp.sum(jnp.where(gt, 1.0, 0.0), axis=1, keepdims=True)
    ri = lax.broadcasted_iota(jnp.int32, (128, 128), 0)
    ci = lax.broadcasted_iota(jnp.int32, (128, 128), 1)
    tri = jnp.where(ri <= ci, 1.0, 0.0).astype(BF16)
    eq_rank = _cumsum_lanes(jnp.where(eq, 1.0, 0.0).astype(BF16), tri)
    sel = gt | (eq & (eq_rank <= need))
    slot = _cumsum_lanes(jnp.where(sel, 1.0, 0.0).astype(BF16), tri) - 1.0
    pos_ref[0] = jnp.where(sel, slot, -1.0).astype(jnp.int32)
    gate_ref[0] = aff


def _router(logits_t, cap):
    bsz, n_e, t = logits_t.shape
    blk = pl.BlockSpec((1, n_e, t), lambda b: (b, 0, 0))
    return pl.pallas_call(
        functools.partial(_router_kernel, cap=cap),
        grid=(bsz,),
        in_specs=[blk],
        out_specs=[blk, blk],
        out_shape=[jax.ShapeDtypeStruct((bsz, n_e, t), jnp.int32), jax.ShapeDtypeStruct((bsz, n_e, t), F32)],
        compiler_params=_params(("parallel",), 32),
        name="router",
    )(logits_t)


EXPERT_ROWS = 256


def _expert_rows(pos_ref, gate_ref, h_ref, o_ref, wg, wu, wd, cap):
    nb, t = h_ref.shape[0], h_ref.shape[1]
    slot_id = lax.broadcasted_iota(jnp.int32, (cap, t), 0)
    rows, gates = [], []
    for i in range(nb):
        hit = slot_id == pos_ref[i, 0]
        rows.append(_dot(jnp.where(hit, 1.0, 0.0).astype(BF16), h_ref[i]).astype(BF16))
        gates.append(jnp.sum(jnp.where(hit, gate_ref[i, 0], 0.0), axis=1, keepdims=True))
    xs = jnp.concatenate(rows, axis=0) if nb > 1 else rows[0]
    gate = jnp.concatenate(gates, axis=0) if nb > 1 else gates[0]
    a = _dot(xs, wg)
    u = _dot(xs, wu)
    ye = (_dot((_silu(a) * u).astype(BF16), wd) * gate).astype(o_ref.dtype)
    for i in range(nb):
        o_ref[i, 0] = ye[i * cap:(i + 1) * cap]


def _expert_kernel(pos_ref, gate_ref, h_ref, wg_ref, wu_ref, wd_ref, o_ref, *rest, cap, n_e, emit):
    g, c = pl.program_id(0), pl.program_id(1)
    wg_s, wu_s, wd_s = rest[-3:]

    @pl.when(g < n_e)
    def _():
        slot = g % 2
        for w_ref, w_s, w_out in zip((wg_ref, wu_ref, wd_ref), (wg_s, wu_s, wd_s), rest[:-3] if emit else (None,) * 3):
            rows = w_ref.shape[0]
            chunk = w_ref[...].astype(BF16)
            w_s[slot, pl.ds(pl.multiple_of(c * rows, rows), rows), :] = chunk
            if w_out is not None:
                w_out[...] = chunk

    @pl.when(g > 0)
    def _():
        cur = (g - 1) % 2
        _expert_rows(pos_ref, gate_ref, h_ref, o_ref, wg_s[cur], wu_s[cur], wd_s[cur], cap)


def _experts(pos, gate, h, wg, wu, wd, layer, cap, emit):
    bsz, t, d = h.shape
    _, n_e, _, ff = wg.shape
    nc = bsz
    assert cap == EXPERT_ROWS and d % nc == 0 and ff % nc == 0

    def sample(g, c):
        return jnp.where(g == 0, 0, c)

    def act(g, c):
        return (sample(g, c), jnp.maximum(g - 1, 0), 0, 0)

    def wchunk(g, c):
        return (layer, jnp.minimum(g, n_e - 1), jnp.where(g < n_e, c, nc - 1), 0)

    def wemit(g, c):
        return wchunk(g, c)[1:]

    row = pl.BlockSpec((1, 1, 1, t), act)
    out_specs = [pl.BlockSpec((1, 1, cap, d), act)]
    out_shape = [jax.ShapeDtypeStruct((bsz, n_e, cap, d), BF16)]
    if emit:
        out_specs += [pl.BlockSpec((None, d // nc, ff), wemit), pl.BlockSpec((None, d // nc, ff), wemit),
                      pl.BlockSpec((None, ff // nc, d), wemit)]
        out_shape += [jax.ShapeDtypeStruct((n_e, d, ff), BF16), jax.ShapeDtypeStruct((n_e, d, ff), BF16),
                      jax.ShapeDtypeStruct((n_e, ff, d), BF16)]
    outs = pl.pallas_call(
        functools.partial(_expert_kernel, cap=cap, n_e=n_e, emit=emit),
        grid=(n_e + 1, nc),
        in_specs=[row, row, pl.BlockSpec((1, t, d), lambda g, c: (sample(g, c), 0, 0)),
                  pl.BlockSpec((None, None, d // nc, ff), wchunk),
                  pl.BlockSpec((None, None, d // nc, ff), wchunk),
                  pl.BlockSpec((None, None, ff // nc, d), wchunk)],
        out_specs=out_specs,
        out_shape=out_shape,
        scratch_shapes=[pltpu.VMEM((2, d, ff), BF16), pltpu.VMEM((2, d, ff), BF16), pltpu.VMEM((2, ff, d), BF16)],
        compiler_params=_params(("arbitrary", "arbitrary"), 58),
        name="experts",
    )(pos.reshape(bsz, n_e, 1, t), gate.reshape(bsz, n_e, 1, t), h, wg, wu, wd)
    return outs[0], tuple(outs[1:])


def _expert_small_kernel(pos_ref, gate_ref, h_ref, wg_ref, wu_ref, wd_ref, o_ref, *, cap):
    _expert_rows(pos_ref, gate_ref, h_ref, o_ref, wg_ref[...], wu_ref[...], wd_ref[...], cap)


def _experts_small(pos, gate, h, wg, wu, wd, cap):
    bsz, t, d = h.shape
    n_e, _, ff = wg.shape
    nb = min(bsz, EXPERT_ROWS // cap)
    assert bsz % nb == 0
    row = pl.BlockSpec((nb, 1, 1, t), lambda e, b: (b, e, 0, 0))
    return pl.pallas_call(
        functools.partial(_expert_small_kernel, cap=cap),
        grid=(n_e, bsz // nb),
        in_specs=[row, row, pl.BlockSpec((nb, t, d), lambda e, b: (b, 0, 0)),
                  pl.BlockSpec((None, d, ff), lambda e, b: (e, 0, 0)),
                  pl.BlockSpec((None, d, ff), lambda e, b: (e, 0, 0)),
                  pl.BlockSpec((None, ff, d), lambda e, b: (e, 0, 0))],
        out_specs=pl.BlockSpec((nb, 1, cap, d), lambda e, b: (b, e, 0, 0)),
        out_shape=jax.ShapeDtypeStruct((bsz, n_e, cap, d), BF16),
        compiler_params=_params(("arbitrary", "arbitrary"), 56),
        name="experts_small",
    )(pos.reshape(bsz, n_e, 1, t), gate.reshape(bsz, n_e, 1, t), h, wg, wu, wd)


def _combine_kernel(pos_ref, ye_ref, x_ref, g2_ref, *rest, cap):
    o_ref = rest[-1]
    n_e = pos_ref.shape[1]
    tb = x_ref.shape[1]
    slot = lax.broadcasted_iota(jnp.int32, (cap, tb), 0)
    hit = jnp.concatenate([jnp.where(slot == pos_ref[0, e:e + 1, :], 1.0, 0.0).astype(BF16) for e in range(n_e)],
                          axis=0)
    xn = x_ref[0] + g2_ref[0] * _dot_tn(hit, ye_ref[0])
    if len(rest) == 2:
        ms = jnp.mean(xn * xn, axis=-1, keepdims=True)
        xn = (xn * lax.rsqrt(ms + EPS)) * rest[0][...]
    o_ref[0] = xn


def _combine(pos, ye, x, g2, cap, tb, final_gain=None):
    bsz, t, d = x.shape
    n_e = pos.shape[1]
    xb = pl.BlockSpec((1, tb, d), lambda b, i: (b, i, 0))
    in_specs = [pl.BlockSpec((1, n_e, tb), lambda b, i: (b, 0, i)),
                pl.BlockSpec((1, n_e * cap, d), lambda b, i: (b, 0, 0)),
                xb, pl.BlockSpec((1, 1, d), lambda b, i: (b, 0, 0))]
    args = [pos, ye.reshape(bsz, n_e * cap, d), x, g2]
    if final_gain is not None:
        in_specs.append(pl.BlockSpec((1, d), lambda b, i: (0, 0)))
        args.append(final_gain.reshape(1, d))
    return pl.pallas_call(
        functools.partial(_combine_kernel, cap=cap),
        grid=(bsz, t // tb),
        in_specs=in_specs,
        out_specs=xb,
        out_shape=jax.ShapeDtypeStruct((bsz, t, d), F32),
        compiler_params=_params(("parallel", "arbitrary"), 56),
        name="combine",
    )(*args)


def _moe(x, h, logits, g2, lw, final_gain=None, emit=False, weights_bf16=None):
    bsz, t, d = x.shape
    cap = EC_CAPACITY_FACTOR * t // N_EXPERTS
    pos, gate = _router(jnp.swapaxes(logits, 1, 2), cap)
    if weights_bf16 is None:
        ye, emitted = _experts(pos, gate, h, lw["w_gate"], lw["w_up"], lw["w_down"], lw["layer"], cap, emit)
    else:
        ye, emitted = _experts_small(pos, gate, h, *weights_bf16, cap), ()
    return _combine(pos, ye, x, g2, cap, min(t, 512), final_gain), emitted


def _layer(x, xc, mods, mods_c, lw, need_ctx, final_gain):
    sh1, sc1, g1, sh2, sc2, g2 = mods
    sh1c, sc1c, g1c, sh2c, sc2c, g2c = mods_c
    names_lat = tuple(p[0] for p in _PIECES)
    p = _inproj(x, lw["norm_mix"], sc1, sh1, lw["w_in"], rope=True, names=names_lat, tm=TOKEN_TILE)
    pc = _inproj(xc, lw["norm_mix"], sc1c, sh1c, lw["w_in"], rope=False,
                 names=names_lat if need_ctx else _CTX_KV_ONLY, tm=CTX_TOKEN_TILE)

    y_ret, yc_ret = _retention(lw["ret_decay"], p, pc, need_ctx)
    y_conv = _short_conv(lw["conv_w"], p["conv_b"], p["conv_c"], p["conv_h"])
    y_swa = _swa(lw["swa_sink"], p, pc)
    y_na = _na(_na_bias(lw["na_rpb"]), p, pc)
    x, h, logits = _outproj((y_ret, y_conv, y_swa, y_na), lw["w_out"], x, g1, lw["norm_ffn"], sc2, sh2,
                            lw["w_router"], tm=OUT_TOKEN_TILE)
    x, w_bf16 = _moe(x, h, logits, g2, lw, final_gain, emit=need_ctx)
    if need_ctx:
        yc_conv = _short_conv(lw["conv_w"], pc["conv_b"], pc["conv_c"], pc["conv_h"])
        yc_swa, yc_na = _ctx_attn(lw["swa_sink"], pc)
        xc, hc, logits_c = _outproj((yc_ret, yc_conv, yc_swa, yc_na), lw["w_out"], xc, g1c, lw["norm_ffn"],
                                    sc2c, sh2c, lw["w_router"], tm=CTX_TOKEN_TILE)
        xc, _ = _moe(xc, hc, logits_c, g2c, lw, weights_bf16=w_bf16)
    return x, xc


def kernel(x, c, ctx, c_ctx, w_ada, b_ada, norm_mix, norm_ffn, w_in, w_out, ret_decay_fwd, ret_decay_bwd,
           conv_w, swa_sink, na_rpb, w_router, w_gate, w_up, w_down, norm_final):
    bsz, _, d = x.shape
    depth = w_ada.shape[0]
    assert bsz + 1 <= ADA_ROWS
    cs = jnp.zeros((ADA_ROWS, d), F32).at[:bsz].set(c).at[bsz].set(c_ctx)
    ada = _ada(cs, w_ada, b_ada)
    w_out_b = w_out.astype(BF16)
    xc = ctx
    for l in range(depth):
        mods = tuple(ada[l, :bsz, i * d:(i + 1) * d].reshape(bsz, 1, d) for i in range(6))
        mods_c = tuple(jnp.broadcast_to(ada[l, bsz, i * d:(i + 1) * d].reshape(1, 1, d), (bsz, 1, d))
                       for i in range(6))
        lw = dict(norm_mix=norm_mix[l], norm_ffn=norm_ffn[l], w_in=w_in[l].astype(BF16),
                  w_out=w_out_b[l], ret_decay=jnp.stack([ret_decay_fwd[l], ret_decay_bwd[l]]), conv_w=conv_w[l],
                  swa_sink=swa_sink[l], na_rpb=na_rpb[l], w_router=w_router[l],
                  w_gate=w_gate, w_up=w_up, w_down=w_down, layer=l)
        last = l == depth - 1
        x, xc = _layer(x, xc, mods, mods_c, lw, need_ctx=not last, final_gain=norm_final if last else None)
    return x
```

```python
import functools

import numpy as np
import jax
import jax.numpy as jnp
from jax import lax
from jax.experimental import pallas as pl
from jax.experimental.pallas import tpu as pltpu

D_MODEL = 2048
DEPTH = 2
GRID_W = 64
HEAD_DIM = 64
GROUP_WIDTH = D_MODEL // 4
RET_CHUNK = 128
SWA_HEADS = 8
SWA_KV_HEADS = 2
SWA_WINDOW = 128
SWA_BLOCK = 128
NA_HEADS = 8
NA_ROWS = 8
NA_COLS = 16
N_EXPERTS = 16
EXPERT_FF = D_MODEL // 2
EC_CAPACITY_FACTOR = 2
ROPE_BASE = 10000.0
EPS = 1e-6
NEG_INF = -1e30
F32 = jnp.float32
BF16 = jnp.bfloat16
ADA_ROWS = 16
QK_SCALE = HEAD_DIM ** -0.5
LOG2E = 1.4426950408889634
PAIR = 2 * HEAD_DIM
TOKEN_TILE = 512
OUT_TOKEN_TILE = 256
CTX_TOKEN_TILE = 256

_PIECES = (
    ("ret_q", 0, 512, True, 1.0, False),
    ("ret_k", 512, 512, True, QK_SCALE, False),
    ("ret_v", 1024, 512, False, 1.0, False),
    ("ret_g", 1536, 512, False, 1.0, False),
    ("conv_b", 2048, 512, False, 1.0, False),
    ("conv_c", 2560, 512, False, 1.0, False),
    ("conv_h", 3072, 512, False, 1.0, False),
    ("swa_q", 3584, 512, True, QK_SCALE * LOG2E, False),
    ("swa_k", 4096, 128, True, 1.0, True),
    ("swa_v", 4224, 128, False, 1.0, True),
    ("na_q", 4352, 512, False, QK_SCALE * LOG2E, False),
    ("na_k", 4864, 512, False, 1.0, False),
    ("na_v", 5376, 512, False, 1.0, False),
)
IN_COLS = 5888
_CTX_KV_ONLY = ("ret_k", "ret_v", "swa_k", "swa_v", "na_k", "na_v")


def _params(sem, vmem_mb):
    return pltpu.CompilerParams(dimension_semantics=sem, vmem_limit_bytes=vmem_mb << 20)


def _silu(x):
    return x * jax.nn.sigmoid(x)


def _dot(a, b):
    return jnp.dot(a, b, preferred_element_type=F32)


def _dot_nt(a, b):
    return lax.dot_general(a, b, (((1,), (1,)), ((), ())), preferred_element_type=F32)


def _dot_tn(a, b):
    return lax.dot_general(a, b, (((0,), (0,)), ((), ())), preferred_element_type=F32)


def _first_head_lanes():
    return lax.broadcasted_iota(jnp.int32, (1, PAIR), 1) < HEAD_DIM


def _keep_head(x, lo, j):
    return jnp.where(lo if j == 0 else jnp.logical_not(lo), x, jnp.zeros_like(x))


def _ada_kernel(c_ref, w_ref, b_ref, o_ref):
    s = _silu(c_ref[...]).astype(BF16)
    o_ref[0] = _dot(s, w_ref[0].astype(BF16)) + b_ref[0]


def _ada(cs, w_ada, b_ada):
    depth, d, n = w_ada.shape
    tn = 1024
    return pl.pallas_call(
        _ada_kernel,
        grid=(depth, n // tn),
        in_specs=[
            pl.BlockSpec((ADA_ROWS, d), lambda l, j: (0, 0)),
            pl.BlockSpec((1, d, tn), lambda l, j: (l, 0, j)),
            pl.BlockSpec((1, 1, tn), lambda l, j: (l, 0, j)),
        ],
        out_specs=pl.BlockSpec((1, ADA_ROWS, tn), lambda l, j: (l, 0, j)),
        out_shape=jax.ShapeDtypeStruct((depth, ADA_ROWS, n), F32),
        compiler_params=_params(("arbitrary", "arbitrary"), 40),
        name="ada",
    )(cs, w_ada, b_ada.reshape(depth, 1, n))


def _rope_tables(seq):
    half = HEAD_DIM // 2
    nf = half // 2
    t = np.arange(seq)
    row = (t // GRID_W).astype(np.float32)
    col = (t % GRID_W).astype(np.float32)
    inv = (np.float32(ROPE_BASE) ** (-np.arange(nf, dtype=np.float32) / np.float32(nf))).astype(np.float32)
    ang = np.concatenate([row[:, None] * inv, col[:, None] * inv], axis=-1).astype(np.float32)
    cos, sin = np.cos(ang), np.sin(ang)
    cos64 = np.concatenate([cos, cos], axis=-1)
    sin64 = np.concatenate([-sin, sin], axis=-1)
    return (np.tile(cos64, (1, 2)).astype(np.float32), np.tile(sin64, (1, 2)).astype(np.float32))


def _rope(z, cos, sin_signed):
    w = z.shape[1]
    reps = w // 128
    cos_w = jnp.concatenate([cos] * reps, axis=1) if reps > 1 else cos
    sin_w = jnp.concatenate([sin_signed] * reps, axis=1) if reps > 1 else sin_signed
    lane = lax.broadcasted_iota(jnp.int32, (1, w), 1)
    first_half = (lane & (HEAD_DIM - 1)) < (HEAD_DIM // 2)
    up = pltpu.roll(z, w - HEAD_DIM // 2, 1)
    dn = pltpu.roll(z, HEAD_DIM // 2, 1)
    return z * cos_w + jnp.where(first_half, up, dn) * sin_w


def _rms_modulate(x, gain, scale, shift):
    ms = jnp.mean(x * x, axis=-1, keepdims=True)
    return (x * lax.rsqrt(ms + EPS)) * (gain * (1.0 + scale)) + shift


def _inproj_kernel(*refs, pieces, rope):
    if rope:
        x_ref, gain_ref, sc_ref, sh_ref, cos_ref, sin_ref, w_ref = refs[:7]
        o_refs = refs[7:]
    else:
        x_ref, gain_ref, sc_ref, sh_ref, w_ref = refs[:5]
        o_refs = refs[5:]
    hb = _rms_modulate(x_ref[0], gain_ref[...], sc_ref[0], sh_ref[0]).astype(BF16)
    for (_, off, width, do_rope, scale, dup), o_ref in zip(pieces, o_refs):
        z = _dot(hb, w_ref[:, off:off + width])
        if rope and do_rope:
            z = _rope(z, cos_ref[...], sin_ref[...])
        if scale != 1.0:
            z = z * scale
        if dup:
            heads = [z[:, j * HEAD_DIM:(j + 1) * HEAD_DIM] for j in range(width // HEAD_DIM)]
            z = jnp.concatenate([h for h in heads for _ in range(2)], axis=1)
        o_ref[0] = z.astype(BF16)


def _inproj(x, gain, sc, sh, w_in, *, rope, names, tm):
    bsz, seq, d = x.shape
    pieces = tuple(p for p in _PIECES if p[0] in names)
    widths = [p[2] * (2 if p[5] else 1) for p in pieces]
    vec = pl.BlockSpec((1, 1, d), lambda b, i: (b, 0, 0))
    in_specs = [pl.BlockSpec((1, tm, d), lambda b, i: (b, i, 0)),
                pl.BlockSpec((1, d), lambda b, i: (0, 0)), vec, vec]
    args = [x, gain.reshape(1, d), sc, sh]
    if rope:
        cos, sin = _rope_tables(seq)
        in_specs += [pl.BlockSpec((tm, 128), lambda b, i: (i, 0))] * 2
        args += [jnp.asarray(cos), jnp.asarray(sin)]
    in_specs.append(pl.BlockSpec((d, IN_COLS), lambda b, i: (0, 0), pipeline_mode=pl.Buffered(1)))
    args.append(w_in)
    outs = pl.pallas_call(
        functools.partial(_inproj_kernel, pieces=pieces, rope=rope),
        grid=(bsz, seq // tm),
        in_specs=in_specs,
        out_specs=[pl.BlockSpec((1, tm, w), lambda b, i: (b, i, 0)) for w in widths],
        out_shape=[jax.ShapeDtypeStruct((bsz, seq, w), BF16) for w in widths],
        compiler_params=_params(("parallel", "parallel"), 56),
        name="inproj_lat" if rope else "inproj_ctx",
    )(*args)
    return {p[0]: o for p, o in zip(pieces, outs)}


def _log_sigmoid(x):
    return jnp.minimum(x, 0.0) - jnp.log(1.0 + jnp.exp(-jnp.abs(x)))


RET_PAIRS = 2


def _ret_kernel(dec_ref, q_ref, k_ref, v_ref, g_ref, qc_ref, kc_ref, vc_ref, gc_ref, *rest, need_ctx):
    if need_ctx:
        y_ref, yc_ref, sb_ref = rest
    else:
        (y_ref, sb_ref), yc_ref = rest, None
    c = RET_CHUNK
    hd = HEAD_DIM
    n_lat = q_ref.shape[1] // c
    n_ctx = qc_ref.shape[1] // c
    n_pairs = q_ref.shape[2] // PAIR
    first_head = 2 * n_pairs * pl.program_id(1)

    lo = _first_head_lanes()
    ri = lax.broadcasted_iota(jnp.int32, (c, c), 0)
    ci = lax.broadcasted_iota(jnp.int32, (c, c), 1)
    diff = (ri - ci).astype(F32)
    same_head = (ri < hd) == (ci < hd)
    pos = lax.broadcasted_iota(jnp.int32, (c, PAIR), 0).astype(F32)

    consts = []
    for p in range(n_pairs):
        lg_f = [_log_sigmoid(jnp.full((1, c), dec_ref[0, first_head + 2 * p + j], F32)) for j in range(2)]
        lg_b = [_log_sigmoid(jnp.full((1, c), dec_ref[1, first_head + 2 * p + j], F32)) for j in range(2)]
        dmat = jnp.concatenate(
            [jnp.where(diff >= 0.0, jnp.exp(lg_f[j] * jnp.maximum(diff, 0.0)),
                       jnp.exp(lg_b[j] * jnp.maximum(-diff, 0.0))) for j in range(2)], axis=0)
        lf = jnp.where(lo, lg_f[0], lg_f[1])
        lb = jnp.where(lo, lg_b[0], lg_b[1])
        consts.append(dict(dmat=dmat, xi_f=jnp.exp(lf * (pos + 1.0)), xi_b=jnp.exp(lb * (c - pos)),
                           ze_f=jnp.exp(lf * (c - 1.0 - pos)), ze_b=jnp.exp(lb * pos),
                           cd_f=jnp.exp(lf * float(c)), cd_b=jnp.exp(lb * float(c))))

    def rows(ref, r0, p):
        return ref[0, pl.ds(r0, c), p * PAIR:(p + 1) * PAIR]

    def kv_outer(k, v, zeta):
        kz = (k.astype(F32) * zeta).astype(BF16)
        return jnp.where(same_head, _dot_tn(kz, v), 0.0)

    def half_sum(x):
        s_lo = jnp.sum(jnp.where(lo, x, 0.0), axis=-1, keepdims=True)
        s_all = jnp.sum(x, axis=-1, keepdims=True)
        return jnp.where(lo, s_lo, s_all - s_lo)

    def bwd_chunk(k_r, v_r, r0, slot, states):
        new = []
        for p in range(n_pairs):
            sb_ref[slot, p] = states[p]
            new.append(states[p] * consts[p]["cd_b"] + kv_outer(rows(k_r, r0, p), rows(v_r, r0, p), consts[p]["ze_b"]))
        return tuple(new)

    def fwd_chunk(q_r, k_r, v_r, g_r, o_r, r0, slot, states):
        ks = [rows(k_r, r0, p) for p in range(n_pairs)]
        vs = [rows(v_r, r0, p) for p in range(n_pairs)]
        if o_r is not None:
            qs = [rows(q_r, r0, p) for p in range(n_pairs)]
            decayed = []
            for p in range(n_pairs):
                q2 = jnp.concatenate([_keep_head(qs[p], lo, 0), _keep_head(qs[p], lo, 1)], axis=0)
                decayed.append((_dot_nt(q2, ks[p]) * consts[p]["dmat"]).astype(BF16))
            ys = []
            for p in range(n_pairs):
                cp, a, qf = consts[p], decayed[p], qs[p].astype(F32)
                lhs = jnp.concatenate([a[:c], a[c:], (qf * cp["xi_f"]).astype(BF16), (qf * cp["xi_b"]).astype(BF16)],
                                      axis=1)
                rhs = jnp.concatenate([_keep_head(vs[p], lo, 0), _keep_head(vs[p], lo, 1), states[p].astype(BF16),
                                       sb_ref[slot, p].astype(BF16)], axis=0)
                ys.append(_dot(lhs, rhs))
            for p in range(n_pairs):
                yc = ys[p] - half_sum(ys[p]) * (1.0 / hd)
                var = half_sum(yc * yc) * (1.0 / hd)
                out = _silu(rows(g_r, r0, p).astype(F32)) * (yc * lax.rsqrt(var + EPS))
                o_r[0, pl.ds(r0, c), p * PAIR:(p + 1) * PAIR] = out.astype(o_r.dtype)
        return tuple(states[p] * consts[p]["cd_f"] + kv_outer(ks[p], vs[p], consts[p]["ze_f"])
                     for p in range(n_pairs))

    zero = tuple(jnp.zeros((PAIR, PAIR), F32) for _ in range(n_pairs))
    st = zero
    for cc in range(n_ctx - 1, -1, -1):
        st = bwd_chunk(kc_ref, vc_ref, cc * c, cc, st)

    def bwd_body(i, s):
        cl = n_lat - 1 - i
        return bwd_chunk(k_ref, v_ref, pl.multiple_of(cl * c, c), n_ctx + cl, s)

    lax.fori_loop(0, n_lat, bwd_body, st, unroll=4)

    st = zero
    for cc in range(n_ctx):
        st = fwd_chunk(qc_ref, kc_ref, vc_ref, gc_ref, yc_ref, cc * c, cc, st)

    def fwd_body(cl, s):
        return fwd_chunk(q_ref, k_ref, v_ref, g_ref, y_ref, pl.multiple_of(cl * c, c), n_ctx + cl, s)

    lax.fori_loop(0, n_lat, fwd_body, st, unroll=4)


def _retention(dec, p, pc, need_ctx):
    bsz, seq, w = p["ret_q"].shape
    t = pc["ret_k"].shape[1]
    bw = RET_PAIRS * PAIR
    lat = pl.BlockSpec((1, seq, bw), lambda b, h: (b, 0, h))
    ctx = pl.BlockSpec((1, t, bw), lambda b, h: (b, 0, h))
    out_specs, out_shape = [lat], [jax.ShapeDtypeStruct((bsz, seq, w), BF16)]
    if need_ctx:
        out_specs.append(ctx)
        out_shape.append(jax.ShapeDtypeStruct((bsz, t, w), BF16))
        qc, gc = pc["ret_q"], pc["ret_g"]
    else:
        qc, gc = pc["ret_k"], pc["ret_k"]
    n_chunks = (seq + t) // RET_CHUNK
    outs = pl.pallas_call(
        functools.partial(_ret_kernel, need_ctx=need_ctx),
        grid=(bsz, w // bw),
        in_specs=[pl.BlockSpec(memory_space=pltpu.SMEM), lat, lat, lat, lat, ctx, ctx, ctx, ctx],
        out_specs=out_specs,
        out_shape=out_shape,
        scratch_shapes=[pltpu.VMEM((n_chunks, RET_PAIRS, PAIR, PAIR), F32)],
        compiler_params=_params(("parallel", "parallel"), 32),
        name="retention",
    )(dec, p["ret_q"], p["ret_k"], p["ret_v"], p["ret_g"], qc, pc["ret_k"], pc["ret_v"], gc)
    return (outs[0], outs[1]) if need_ctx else (outs[0], None)


def _conv_kernel(w_ref, b_ref, c_ref, h_ref, o_ref):
    seq = b_ref.shape[1]
    rows = min(256, seq)
    halo = 16
    w0, w1, w2 = w_ref[0:1, :], w_ref[1:2, :], w_ref[2:3, :]
    for r0 in range(0, seq, rows):
        lo, hi = max(0, r0 - halo), min(seq, r0 + rows + halo)
        u = c_ref[0, lo:hi, :].astype(F32) * h_ref[0, lo:hi, :].astype(F32)
        n = hi - lo
        t = lo + lax.broadcasted_iota(jnp.int32, (n, 1), 0)
        prev = jnp.where(t == 0, 0.0, pltpu.roll(u, 1, 0))
        nxt = jnp.where(t == seq - 1, 0.0, pltpu.roll(u, n - 1, 0))
        y = prev * w0 + u * w1 + nxt * w2
        y = y[r0 - lo:r0 - lo + rows]
        o_ref[0, r0:r0 + rows, :] = (b_ref[0, r0:r0 + rows, :].astype(F32) * y).astype(o_ref.dtype)


def _short_conv(conv_w, b, c, h):
    bsz, seq, w = b.shape
    blk = pl.BlockSpec((1, seq, w), lambda i: (i, 0, 0))
    return pl.pallas_call(
        _conv_kernel,
        grid=(bsz,),
        in_specs=[pl.BlockSpec((3, w), lambda i: (0, 0)), blk, blk, blk],
        out_specs=blk,
        out_shape=jax.ShapeDtypeStruct((bsz, seq, w), BF16),
        compiler_params=_params(("parallel",), 48),
        name="short_conv",
    )(conv_w, b, c, h)


def _softmax_pv_t(scores_t, values, extra=None):
    mx = functools.reduce(jnp.maximum, [jnp.max(s, axis=0, keepdims=True) for s in scores_t])
    if extra is not None:
        mx = jnp.maximum(mx, extra)
    den = jnp.exp2(extra - mx) if extra is not None else 0.0
    acc = 0.0
    for s, v in zip(scores_t, values):
        p = jnp.exp2(s - mx)
        den = den + jnp.sum(p, axis=0, keepdims=True)
        acc = acc + _dot_tn(v, p.astype(BF16))
    return (acc / den).T


def _swa_scores(q_ref, j, lo, win_keys, valid_t, ctx_keys):
    grp = SWA_HEADS // SWA_KV_HEADS
    qs = []
    for pp in range(grp // 2):
        qp = q_ref[0, :, (j * grp // 2 + pp) * PAIR:(j * grp // 2 + pp + 1) * PAIR]
        qs += [_keep_head(qp, lo, 0), _keep_head(qp, lo, 1)]
    qg = jnp.concatenate(qs, axis=0)
    scores = []
    if win_keys is not None:
        scores.append(jnp.where(valid_t, _dot_nt(win_keys, qg), NEG_INF))
    scores.append(_dot_nt(ctx_keys, qg))
    return scores


def _swa_outputs(scores, values, sink_ref, j, lo, m):
    grp = SWA_HEADS // SWA_KV_HEADS
    sink = jnp.concatenate([jnp.full((1, m), sink_ref[j * grp + g] * LOG2E, F32) for g in range(grp)], axis=1)
    o = _softmax_pv_t(scores, values, sink)
    return [jnp.where(lo, o[2 * pp * m:(2 * pp + 1) * m], o[(2 * pp + 1) * m:(2 * pp + 2) * m])
            for pp in range(grp // 2)]


def _swa_kernel(sink_ref, q_ref, k_ref, v_ref, kc_ref, vc_ref, o_ref):
    blk = SWA_BLOCK
    win = 3 * blk
    seq = k_ref.shape[1]
    grp = SWA_HEADS // SWA_KV_HEADS
    i = pl.program_id(1)
    start = pl.multiple_of(jnp.clip((i - 1) * blk, 0, seq - win), blk)
    lo = _first_head_lanes()
    kpos = start + lax.broadcasted_iota(jnp.int32, (win, blk), 0)
    qpos = i * blk + lax.broadcasted_iota(jnp.int32, (win, blk), 1)
    valid = jnp.abs(kpos - qpos) <= SWA_WINDOW
    valid = jnp.concatenate([valid] * grp, axis=1)
    kv_lanes = [slice(j * PAIR, (j + 1) * PAIR) for j in range(SWA_KV_HEADS)]
    scores = [_swa_scores(q_ref, j, lo, k_ref[0, pl.ds(start, win), sl], valid, kc_ref[0, :, sl])
              for j, sl in enumerate(kv_lanes)]
    for j, sl in enumerate(kv_lanes):
        outs = _swa_outputs(scores[j], [v_ref[0, pl.ds(start, win), sl], vc_ref[0, :, sl]], sink_ref, j, lo, blk)
        for pp, o in enumerate(outs):
            o_ref[0, :, (j * grp // 2 + pp) * PAIR:(j * grp // 2 + pp + 1) * PAIR] = o.astype(o_ref.dtype)


def _swa(sink, p, pc):
    bsz, seq, w = p["swa_q"].shape
    t = pc["swa_k"].shape[1]
    kvw = SWA_KV_HEADS * PAIR
    kv = pl.BlockSpec((1, seq, kvw), lambda b, i: (b, 0, 0))
    kvc = pl.BlockSpec((1, t, kvw), lambda b, i: (b, 0, 0))
    qb = pl.BlockSpec((1, SWA_BLOCK, w), lambda b, i: (b, i, 0))
    return pl.pallas_call(
        _swa_kernel,
        grid=(bsz, seq // SWA_BLOCK),
        in_specs=[pl.BlockSpec(memory_space=pltpu.SMEM), qb, kv, kv, kvc, kvc],
        out_specs=qb,
        out_shape=jax.ShapeDtypeStruct((bsz, seq, w), BF16),
        compiler_params=_params(("parallel", "parallel"), 32),
        name="swa",
    )(sink, p["swa_q"], p["swa_k"], p["swa_v"], pc["swa_k"], pc["swa_v"])


NA_GROUP = 4
NA_BAND = NA_ROWS + NA_GROUP - 1
NA_GRID_ROWS = 32
_NA_CFGS = ((0, 0), (4, 0), (8, 4), (28, 21))


def _na_bias_kernel(rpb_ref, o_ref):
    hp = pl.program_id(0)
    gw, wc = GRID_W, NA_COLS
    rows = NA_GRID_ROWS
    ck = lax.broadcasted_iota(jnp.int32, (gw, gw), 0)
    cq = lax.broadcasted_iota(jnp.int32, (gw, gw), 1)
    col_start = jnp.clip(cq - wc // 2, 0, gw - wc)
    col_ok = (ck >= col_start) & (ck < col_start + wc)
    dc = jnp.clip(ck - cq, -(wc - 1), wc - 1) + (wc - 1)
    neg = jnp.full((gw, gw), NEG_INF, F32)
    per_dr = []
    for j in range(2):
        tiles = []
        for dr in range(2 * NA_ROWS - 1):
            t = jnp.zeros((gw, gw), F32)
            for d in range(2 * wc - 1):
                t = jnp.where(dc == d, rpb_ref[2 * hp + j, dr, d] * LOG2E, t)
            tiles.append(jnp.where(col_ok, t, NEG_INF))
        per_dr.append(tiles)
    for c, (r0, start) in enumerate(_NA_CFGS):
        for jp in range(NA_BAND):
            krow = start + jp
            blocks = []
            for j in range(2):
                for i in range(NA_GROUP):
                    r = r0 + i
                    band0 = min(max(r - NA_ROWS // 2, 0), rows - NA_ROWS)
                    blocks.append(per_dr[j][krow - r + NA_ROWS - 1] if band0 <= krow < band0 + NA_ROWS else neg)
            o_ref[c, 0, jp * gw:(jp + 1) * gw, :] = jnp.concatenate(blocks, axis=1)


def _na_bias(rpb):
    n_cfg = len(_NA_CFGS)
    blk = (n_cfg, 1, NA_BAND * GRID_W, 2 * NA_GROUP * GRID_W)
    return pl.pallas_call(
        _na_bias_kernel,
        grid=(NA_HEADS // 2,),
        in_specs=[pl.BlockSpec(memory_space=pltpu.SMEM)],
        out_specs=pl.BlockSpec(blk, lambda h: (0, h, 0, 0)),
        out_shape=jax.ShapeDtypeStruct((n_cfg, NA_HEADS // 2) + blk[2:], F32),
        compiler_params=_params(("parallel",), 32),
        name="na_bias",
    )(rpb)


def _na_scores(qp, lo, keys, bias):
    q2 = jnp.concatenate([_keep_head(qp, lo, 0), _keep_head(qp, lo, 1)], axis=0)
    scores = [_dot_nt(k, q2) for k in keys]
    if bias is not None:
        scores[0] = scores[0] + bias
    return scores


def _na_pair(qp, lo, keys, values, bias):
    m = qp.shape[0]
    o = _softmax_pv_t(_na_scores(qp, lo, keys, bias), values)
    return jnp.where(lo, o[:m], o[m:])


def _na_kernel(q_ref, k_ref, v_ref, kc_ref, vc_ref, bias_ref, o_ref):
    rows = k_ref.shape[1] // GRID_W
    nk = NA_BAND * GRID_W
    g = pl.program_id(1)
    first = jnp.clip(g * NA_GROUP - NA_ROWS // 2, 0, rows - NA_BAND)
    st = pl.multiple_of(first * GRID_W, GRID_W)
    lo = _first_head_lanes()
    m = q_ref.shape[1]
    pairs = [slice(hp * PAIR, (hp + 1) * PAIR) for hp in range(NA_HEADS // 2)]
    scores = [_na_scores(q_ref[0, :, sl], lo, [k_ref[0, pl.ds(st, nk), sl], kc_ref[0, :, sl]], bias_ref[0, hp])
              for hp, sl in enumerate(pairs)]
    for hp, sl in enumerate(pairs):
        o = _softmax_pv_t(scores[hp], [v_ref[0, pl.ds(st, nk), sl], vc_ref[0, :, sl]])
        o_ref[0, :, sl] = jnp.where(lo, o[:m], o[m:]).astype(o_ref.dtype)


def _na(bias, p, pc):
    bsz, seq, w = p["na_q"].shape
    t = pc["na_k"].shape[1]
    rows = seq // GRID_W
    assert rows == NA_GRID_ROWS and bias.shape[0] == len(_NA_CFGS)
    n_groups = rows // NA_GROUP
    kv = pl.BlockSpec((1, seq, w), lambda b, g: (b, 0, 0))
    kvc = pl.BlockSpec((1, t, w), lambda b, g: (b, 0, 0))
    qb = pl.BlockSpec((1, NA_GROUP * GRID_W, w), lambda b, g: (b, g, 0))

    def cfg(b, g):
        return (jnp.where(g < 2, g, jnp.where(g < n_groups - 1, 2, 3)), 0, 0, 0)

    return pl.pallas_call(
        _na_kernel,
        grid=(bsz, n_groups),
        in_specs=[qb, kv, kv, kvc, kvc,
                  pl.BlockSpec((1, NA_HEADS // 2, NA_BAND * GRID_W, 2 * NA_GROUP * GRID_W), cfg)],
        out_specs=qb,
        out_shape=jax.ShapeDtypeStruct((bsz, seq, w), BF16),
        compiler_params=_params(("parallel", "arbitrary"), 48),
        name="na",
    )(p["na_q"], p["na_k"], p["na_v"], pc["na_k"], pc["na_v"], bias)


def _ctx_attn_kernel(sink_ref, sq_ref, sk_ref, sv_ref, nq_ref, nk_ref, nv_ref, so_ref, no_ref):
    grp = SWA_HEADS // SWA_KV_HEADS
    lo = _first_head_lanes()
    for j in range(SWA_KV_HEADS):
        sl = slice(j * PAIR, (j + 1) * PAIR)
        outs = _swa_outputs(_swa_scores(sq_ref, j, lo, None, None, sk_ref[0, :, sl]), [sv_ref[0, :, sl]], sink_ref, j,
                            lo, sq_ref.shape[1])
        for pp, o in enumerate(outs):
            so_ref[0, :, (j * grp // 2 + pp) * PAIR:(j * grp // 2 + pp + 1) * PAIR] = o.astype(so_ref.dtype)
    for hp in range(NA_HEADS // 2):
        sl = slice(hp * PAIR, (hp + 1) * PAIR)
        o = _na_pair(nq_ref[0, :, sl], lo, [nk_ref[0, :, sl]], [nv_ref[0, :, sl]], None)
        no_ref[0, :, sl] = o.astype(no_ref.dtype)


def _ctx_attn(sink, pc):
    bsz, t, w = pc["swa_q"].shape
    kvw = SWA_KV_HEADS * PAIR
    full = pl.BlockSpec((1, t, w), lambda b: (b, 0, 0))
    kv = pl.BlockSpec((1, t, kvw), lambda b: (b, 0, 0))
    return pl.pallas_call(
        _ctx_attn_kernel,
        grid=(bsz,),
        in_specs=[pl.BlockSpec(memory_space=pltpu.SMEM), full, kv, kv, full, full, full],
        out_specs=[full, full],
        out_shape=[jax.ShapeDtypeStruct((bsz, t, w), BF16)] * 2,
        compiler_params=_params(("parallel",), 32),
        name="ctx_attn",
    )(sink, pc["swa_q"], pc["swa_k"], pc["swa_v"], pc["na_q"], pc["na_k"], pc["na_v"])


def _outproj_kernel(yr_ref, yc_ref, ys_ref, yn_ref, w_ref, x_ref, g1_ref, gain_ref, sc_ref, sh_ref, wr_ref,
                    xo_ref, h_ref, lg_ref):
    y = jnp.concatenate([yr_ref[0], yc_ref[0], ys_ref[0], yn_ref[0]], axis=1)
    xn = x_ref[0] + g1_ref[0] * _dot(y, w_ref[...])
    xo_ref[0] = xn
    hb = _rms_modulate(xn, gain_ref[...], sc_ref[0], sh_ref[0]).astype(BF16)
    h_ref[0] = hb
    lg_ref[0] = _dot(hb, wr_ref[...])


def _outproj(ys, w_out, x, g1, gain, sc, sh, w_router, *, tm):
    bsz, seq, d = x.shape
    yb = pl.BlockSpec((1, tm, GROUP_WIDTH), lambda b, i: (b, i, 0))
    xb = pl.BlockSpec((1, tm, d), lambda b, i: (b, i, 0))
    vec = pl.BlockSpec((1, 1, d), lambda b, i: (b, 0, 0))
    return pl.pallas_call(
        _outproj_kernel,
        grid=(bsz, seq // tm),
        in_specs=[yb, yb, yb, yb,
                  pl.BlockSpec((d, d), lambda b, i: (0, 0), pipeline_mode=pl.Buffered(1)),
                  xb, vec, pl.BlockSpec((1, d), lambda b, i: (0, 0)), vec, vec,
                  pl.BlockSpec((d, N_EXPERTS), lambda b, i: (0, 0))],
        out_specs=[xb, xb, pl.BlockSpec((1, tm, N_EXPERTS), lambda b, i: (b, i, 0))],
        out_shape=[jax.ShapeDtypeStruct((bsz, seq, d), F32), jax.ShapeDtypeStruct((bsz, seq, d), BF16),
                   jax.ShapeDtypeStruct((bsz, seq, N_EXPERTS), F32)],
        compiler_params=_params(("parallel", "parallel"), 48),
        name="outproj",
    )(*ys, w_out, x, g1, gain.reshape(1, d), sc, sh, w_router.astype(BF16))


def _cumsum_lanes(x, tri):
    outs = []
    carry = jnp.zeros((x.shape[0], 1), F32)
    for c0 in range(0, x.shape[1], 128):
        cs = _dot(x[:, c0:c0 + 128], tri) + carry
        outs.append(cs)
        carry = cs[:, 127:128]
    return jnp.concatenate(outs, axis=1) if len(outs) > 1 else outs[0]


ROUTER_BOUNDS = 8


def _router_kernel(lg_ref, pos_ref, gate_ref, bounds_ref, *, cap, tb):
    lg = lg_ref[0]
    e = jnp.exp(lg - jnp.max(lg, axis=0, keepdims=True))
    aff = e / jnp.sum(e, axis=0, keepdims=True)
    capf = float(cap)
    lo = jnp.zeros((lg.shape[0], 1), jnp.int32)
    for bit in range(30, -1, -1):
        cand = lo | (1 << bit)
        cnt = jnp.sum(jnp.where(aff >= pltpu.bitcast(cand, F32), 1.0, 0.0), axis=1, keepdims=True)
        lo = jnp.where(cnt >= capf, cand, lo)
    kth = pltpu.bitcast(lo, F32)
    gt = aff > kth
    eq = aff == kth
    need = capf - jnp.sum(jnp.where(gt, 1.0, 0.0), axis=1, keepdims=True)
    ri = lax.broadcasted_iota(jnp.int32, (128, 128), 0)
    ci = lax.broadcasted_iota(jnp.int32, (128, 128), 1)
    tri = jnp.where(ri <= ci, 1.0, 0.0).astype(BF16)
    eq_rank = _cumsum_lanes(jnp.where(eq, 1.0, 0.0).astype(BF16), tri)
    sel = gt | (eq & (eq_rank <= need))
    count = _cumsum_lanes(jnp.where(sel, 1.0, 0.0).astype(BF16), tri)
    pos_ref[0] = jnp.where(sel, count - 1.0, -1.0).astype(jnp.int32)
    gate_ref[0] = aff
    lane = lax.broadcasted_iota(jnp.int32, (lg.shape[0], 128), 1)
    bounds = jnp.zeros((lg.shape[0], 128), F32)
    for i in range(1, lg.shape[1] // tb + 1):
        bounds = jnp.where(lane == i, count[:, i * tb - 1:i * tb], bounds)
    bounds_ref[0] = bounds.astype(jnp.int32)


def _router(logits_t, cap, tb):
    bsz, n_e, t = logits_t.shape
    assert t // tb + 1 <= ROUTER_BOUNDS
    blk = pl.BlockSpec((1, n_e, t), lambda b: (b, 0, 0))
    pos, gate, bounds = pl.pallas_call(
        functools.partial(_router_kernel, cap=cap, tb=tb),
        grid=(bsz,),
        in_specs=[blk],
        out_specs=[blk, blk, pl.BlockSpec((1, n_e, 128), lambda b: (b, 0, 0))],
        out_shape=[jax.ShapeDtypeStruct((bsz, n_e, t), jnp.int32), jax.ShapeDtypeStruct((bsz, n_e, t), F32),
                   jax.ShapeDtypeStruct((bsz, n_e, 128), jnp.int32)],
        compiler_params=_params(("parallel",), 32),
        name="router",
    )(logits_t)
    return pos, gate, bounds[:, :, :ROUTER_BOUNDS].reshape(-1)


EXPERT_ROWS = 256


def _expert_rows(pos_ref, gate_ref, h_ref, o_ref, wg, wu, wd, cap):
    nb, t = h_ref.shape[0], h_ref.shape[1]
    slot_id = lax.broadcasted_iota(jnp.int32, (cap, t), 0)
    rows, gates = [], []
    for i in range(nb):
        hit = slot_id == pos_ref[i, 0]
        rows.append(_dot(jnp.where(hit, 1.0, 0.0).astype(BF16), h_ref[i]).astype(BF16))
        gates.append(jnp.sum(jnp.where(hit, gate_ref[i, 0], 0.0), axis=1, keepdims=True))
    xs = jnp.concatenate(rows, axis=0) if nb > 1 else rows[0]
    gate = jnp.concatenate(gates, axis=0) if nb > 1 else gates[0]
    a = _dot(xs, wg)
    u = _dot(xs, wu)
    ye = (_dot((_silu(a) * u).astype(BF16), wd) * gate).astype(o_ref.dtype)
    for i in range(nb):
        o_ref[i, 0] = ye[i * cap:(i + 1) * cap]


def _expert_kernel(pos_ref, gate_ref, h_ref, wg_ref, wu_ref, wd_ref, o_ref, *rest, cap, n_e, emit):
    g, c = pl.program_id(0), pl.program_id(1)
    wg_s, wu_s, wd_s = rest[-3:]

    @pl.when(g < n_e)
    def _():
        slot = g % 2
        for w_ref, w_s, w_out in zip((wg_ref, wu_ref, wd_ref), (wg_s, wu_s, wd_s), rest[:-3] if emit else (None,) * 3):
            rows = w_ref.shape[0]
            chunk = w_ref[...].astype(BF16)
            w_s[slot, pl.ds(pl.multiple_of(c * rows, rows), rows), :] = chunk
            if w_out is not None:
                w_out[...] = chunk

    @pl.when(g > 0)
    def _():
        cur = (g - 1) % 2
        _expert_rows(pos_ref, gate_ref, h_ref, o_ref, wg_s[cur], wu_s[cur], wd_s[cur], cap)


def _experts(pos, gate, h, wg, wu, wd, layer, cap, emit):
    bsz, t, d = h.shape
    _, n_e, _, ff = wg.shape
    nc = bsz
    assert cap == EXPERT_ROWS and d % nc == 0 and ff % nc == 0

    def sample(g, c):
        return jnp.where(g == 0, 0, c)

    def act(g, c):
        return (sample(g, c), jnp.maximum(g - 1, 0), 0, 0)

    def wchunk(g, c):
        return (layer, jnp.minimum(g, n_e - 1), jnp.where(g < n_e, c, nc - 1), 0)

    def wemit(g, c):
        return wchunk(g, c)[1:]

    row = pl.BlockSpec((1, 1, 1, t), act)
    out_specs = [pl.BlockSpec((1, 1, cap, d), act)]
    out_shape = [jax.ShapeDtypeStruct((bsz, n_e, cap, d), BF16)]
    if emit:
        out_specs += [pl.BlockSpec((None, d // nc, ff), wemit), pl.BlockSpec((None, d // nc, ff), wemit),
                      pl.BlockSpec((None, ff // nc, d), wemit)]
        out_shape += [jax.ShapeDtypeStruct((n_e, d, ff), BF16), jax.ShapeDtypeStruct((n_e, d, ff), BF16),
                      jax.ShapeDtypeStruct((n_e, ff, d), BF16)]
    outs = pl.pallas_call(
        functools.partial(_expert_kernel, cap=cap, n_e=n_e, emit=emit),
        grid=(n_e + 1, nc),
        in_specs=[row, row, pl.BlockSpec((1, t, d), lambda g, c: (sample(g, c), 0, 0)),
                  pl.BlockSpec((None, None, d // nc, ff), wchunk),
                  pl.BlockSpec((None, None, d // nc, ff), wchunk),
                  pl.BlockSpec((None, None, ff // nc, d), wchunk)],
        out_specs=out_specs,
        out_shape=out_shape,
        scratch_shapes=[pltpu.VMEM((2, d, ff), BF16), pltpu.VMEM((2, d, ff), BF16), pltpu.VMEM((2, ff, d), BF16)],
        compiler_params=_params(("arbitrary", "arbitrary"), 58),
        name="experts",
    )(pos.reshape(bsz, n_e, 1, t), gate.reshape(bsz, n_e, 1, t), h, wg, wu, wd)
    return outs[0], tuple(outs[1:])


def _expert_small_kernel(pos_ref, gate_ref, h_ref, wg_ref, wu_ref, wd_ref, o_ref, *, cap):
    _expert_rows(pos_ref, gate_ref, h_ref, o_ref, wg_ref[...], wu_ref[...], wd_ref[...], cap)


def _experts_small(pos, gate, h, wg, wu, wd, cap):
    bsz, t, d = h.shape
    n_e, _, ff = wg.shape
    nb = min(bsz, EXPERT_ROWS // cap)
    assert bsz % nb == 0
    row = pl.BlockSpec((nb, 1, 1, t), lambda e, b: (b, e, 0, 0))
    return pl.pallas_call(
        functools.partial(_expert_small_kernel, cap=cap),
        grid=(n_e, bsz // nb),
        in_specs=[row, row, pl.BlockSpec((nb, t, d), lambda e, b: (b, 0, 0)),
                  pl.BlockSpec((None, d, ff), lambda e, b: (e, 0, 0)),
                  pl.BlockSpec((None, d, ff), lambda e, b: (e, 0, 0)),
                  pl.BlockSpec((None, ff, d), lambda e, b: (e, 0, 0))],
        out_specs=pl.BlockSpec((nb, 1, cap, d), lambda e, b: (b, e, 0, 0)),
        out_shape=jax.ShapeDtypeStruct((bsz, n_e, cap, d), BF16),
        compiler_params=_params(("arbitrary", "arbitrary"), 56),
        name="experts_small",
    )(pos.reshape(bsz, n_e, 1, t), gate.reshape(bsz, n_e, 1, t), h, wg, wu, wd)


COMBINE_TOKENS = 512
COMBINE_WINDOW = 128
SLOT_ALIGN = 16


def _combine_kernel(bounds_ref, pos_ref, ye_ref, x_ref, g2_ref, *rest, cap):
    o_ref = rest[-1]
    n_e = pos_ref.shape[1]
    tb = x_ref.shape[1]
    win = COMBINE_WINDOW

    def finish(scattered):
        xn = x_ref[0] + g2_ref[0] * scattered
        if len(rest) == 2:
            ms = jnp.mean(xn * xn, axis=-1, keepdims=True)
            xn = (xn * lax.rsqrt(ms + EPS)) * rest[0][...]
        o_ref[0] = xn

    def scatter(n_slots, starts):
        slot = lax.broadcasted_iota(jnp.int32, (n_slots, tb), 0)
        hits, rows = [], []
        for e in range(n_e):
            hits.append(jnp.where(slot + starts[e] == pos_ref[0, e:e + 1, :], 1.0, 0.0).astype(BF16))
            if n_slots == cap:
                rows.append(ye_ref[0, e * cap:(e + 1) * cap, :])
            else:
                rows.append(ye_ref[0, pl.ds(pl.multiple_of(e * cap + starts[e], SLOT_ALIGN), n_slots), :])
        return _dot_tn(jnp.concatenate(hits, axis=0), jnp.concatenate(rows, axis=0))

    if cap <= win:
        finish(scatter(cap, [0] * n_e))
        return
    base = (pl.program_id(0) * n_e) * ROUTER_BOUNDS + pl.program_id(1)
    starts, fits = [], None
    for e in range(n_e):
        first, end = bounds_ref[base + e * ROUTER_BOUNDS], bounds_ref[base + e * ROUTER_BOUNDS + 1]
        st = jnp.minimum((first // SLOT_ALIGN) * SLOT_ALIGN, cap - win)
        starts.append(st)
        ok = end - st <= win
        fits = ok if fits is None else fits & ok

    @pl.when(fits)
    def _():
        finish(scatter(win, starts))

    @pl.when(jnp.logical_not(fits))
    def _():
        finish(scatter(cap, [0] * n_e))


def _combine(bounds, pos, ye, x, g2, cap, tb, final_gain=None):
    bsz, t, d = x.shape
    n_e = pos.shape[1]
    assert cap % SLOT_ALIGN == 0 and (cap <= COMBINE_WINDOW or (cap - COMBINE_WINDOW) % SLOT_ALIGN == 0)
    xb = pl.BlockSpec((1, tb, d), lambda b, i: (b, i, 0))
    in_specs = [pl.BlockSpec(memory_space=pltpu.SMEM),
                pl.BlockSpec((1, n_e, tb), lambda b, i: (b, 0, i)),
                pl.BlockSpec((1, n_e * cap, d), lambda b, i: (b, 0, 0)),
                xb, pl.BlockSpec((1, 1, d), lambda b, i: (b, 0, 0))]
    args = [bounds, pos, ye.reshape(bsz, n_e * cap, d), x, g2]
    if final_gain is not None:
        in_specs.append(pl.BlockSpec((1, d), lambda b, i: (0, 0)))
        args.append(final_gain.reshape(1, d))
    return pl.pallas_call(
        functools.partial(_combine_kernel, cap=cap),
        grid=(bsz, t // tb),
        in_specs=in_specs,
        out_specs=xb,
        out_shape=jax.ShapeDtypeStruct((bsz, t, d), F32),
        compiler_params=_params(("parallel", "arbitrary"), 56),
        name="combine",
    )(*args)


def _moe(x, h, logits, g2, lw, final_gain=None, emit=False, weights_bf16=None):
    bsz, t, d = x.shape
    cap = EC_CAPACITY_FACTOR * t // N_EXPERTS
    tb = min(t, COMBINE_TOKENS)
    pos, gate, bounds = _router(jnp.swapaxes(logits, 1, 2), cap, tb)
    if weights_bf16 is None:
        ye, emitted = _experts(pos, gate, h, lw["w_gate"], lw["w_up"], lw["w_down"], lw["layer"], cap, emit)
    else:
        ye, emitted = _experts_small(pos, gate, h, *weights_bf16, cap), ()
    return _combine(bounds, pos, ye, x, g2, cap, tb, final_gain), emitted


def _layer(x, xc, mods, mods_c, lw, need_ctx, final_gain):
    sh1, sc1, g1, sh2, sc2, g2 = mods
    sh1c, sc1c, g1c, sh2c, sc2c, g2c = mods_c
    names_lat = tuple(p[0] for p in _PIECES)
    p = _inproj(x, lw["norm_mix"], sc1, sh1, lw["w_in"], rope=True, names=names_lat, tm=TOKEN_TILE)
    pc = _inproj(xc, lw["norm_mix"], sc1c, sh1c, lw["w_in"], rope=False,
                 names=names_lat if need_ctx else _CTX_KV_ONLY, tm=CTX_TOKEN_TILE)

    y_ret, yc_ret = _retention(lw["ret_decay"], p, pc, need_ctx)
    y_conv = _short_conv(lw["conv_w"], p["conv_b"], p["conv_c"], p["conv_h"])
    y_swa = _swa(lw["swa_sink"], p, pc)
    y_na = _na(_na_bias(lw["na_rpb"]), p, pc)
    x, h, logits = _outproj((y_ret, y_conv, y_swa, y_na), lw["w_out"], x, g1, lw["norm_ffn"], sc2, sh2,
                            lw["w_router"], tm=OUT_TOKEN_TILE)
    x, w_bf16 = _moe(x, h, logits, g2, lw, final_gain, emit=need_ctx)
    if need_ctx:
        yc_conv = _short_conv(lw["conv_w"], pc["conv_b"], pc["conv_c"], pc["conv_h"])
        yc_swa, yc_na = _ctx_attn(lw["swa_sink"], pc)
        xc, hc, logits_c = _outproj((yc_ret, yc_conv, yc_swa, yc_na), lw["w_out"], xc, g1c, lw["norm_ffn"],
                                    sc2c, sh2c, lw["w_router"], tm=CTX_TOKEN_TILE)
        xc, _ = _moe(xc, hc, logits_c, g2c, lw, weights_bf16=w_bf16)
    return x, xc


def kernel(x, c, ctx, c_ctx, w_ada, b_ada, norm_mix, norm_ffn, w_in, w_out, ret_decay_fwd, ret_decay_bwd,
           conv_w, swa_sink, na_rpb, w_router, w_gate, w_up, w_down, norm_final):
    bsz, _, d = x.shape
    depth = w_ada.shape[0]
    assert bsz + 1 <= ADA_ROWS
    cs = jnp.zeros((ADA_ROWS, d), F32).at[:bsz].set(c).at[bsz].set(c_ctx)
    ada = _ada(cs, w_ada, b_ada)
    w_out_b = w_out.astype(BF16)
    xc = ctx
    for l in range(depth):
        mods = tuple(ada[l, :bsz, i * d:(i + 1) * d].reshape(bsz, 1, d) for i in range(6))
        mods_c = tuple(jnp.broadcast_to(ada[l, bsz, i * d:(i + 1) * d].reshape(1, 1, d), (bsz, 1, d))
                       for i in range(6))
        lw = dict(norm_mix=norm_mix[l], norm_ffn=norm_ffn[l], w_in=w_in[l].astype(BF16),
                  w_out=w_out_b[l], ret_decay=jnp.stack([ret_decay_fwd[l], ret_decay_bwd[l]]), conv_w=conv_w[l],
                  swa_sink=swa_sink[l], na_rpb=na_rpb[l], w_router=w_router[l],
                  w_gate=w_gate, w_up=w_up, w_down=w_down, layer=l)
        last = l == depth - 1
        x, xc = _layer(x, xc, mods, mods_c, lw, need_ctx=not last, final_gain=norm_final if last else None)
    return x
```
